```python
import math
import jax, jax.numpy as jnp
from jax import lax
import numpy as np

D_MODEL = 1024
BATCH = 8
SEQ = 8192
DEPTH = 4
DEC_BATCH = 32
DEC_SEQ = 16
PAST_LEN = 4096

CHUNK = 64
Q_BLOCK = 128
CONV_W = 3
D_CONV = D_MODEL // 2
N_HEADS = 8
QK_NOPE = 64
QK_ROPE = 32
V_HEAD = 64
KV_RANK = 128
Q_RANK = 256
D_MIX = D_CONV + N_HEADS * V_HEAD
D_IN = 3 * D_CONV + Q_RANK + KV_RANK + QK_ROPE
D_FF = 2816
ROPE_BASE = 10000.0
EPS = 1e-6
ATTN_SCALE = 1.0 / math.sqrt(QK_NOPE + QK_ROPE)
IN_SPLITS = (D_CONV, 2 * D_CONV, 3 * D_CONV, 3 * D_CONV + Q_RANK, 3 * D_CONV + Q_RANK + KV_RANK)

kernel_name = "hybrid_shortconv_mla_convffn_stream_step"


def rmsnorm(x, g):
    xf = x.astype(jnp.float32)
    y = xf * lax.rsqrt(jnp.mean(xf * xf, axis=-1, keepdims=True) + EPS)
    return (y * g.astype(jnp.float32)).astype(x.dtype)


def causal_dwconv(u, prev, w):
    T = u.shape[1]
    up = jnp.concatenate([prev.astype(u.dtype), u], axis=1)
    y = up[:, 0:T] * w[0]
    for k in range(1, CONV_W):
        y = y + up[:, k:k + T] * w[k]
    return y, up[:, T:]


def rope_tables(pos):
    inv = 1.0 / (ROPE_BASE ** (jnp.arange(0, QK_ROPE, 2, dtype=jnp.float32) / QK_ROPE))
    ang = pos.astype(jnp.float32)[:, None] * inv[None, :]
    return jnp.cos(ang), jnp.sin(ang)


def apply_rope(x, cos, sin):
    xf = x.astype(jnp.float32)
    x1, x2 = xf[..., :QK_ROPE // 2], xf[..., QK_ROPE // 2:]
    return jnp.concatenate([x1 * cos - x2 * sin, x2 * cos + x1 * sin], axis=-1).astype(x.dtype)


def mla_attend(q_lat, q_pe, c_kv, k_pe, q_pos, k_pos):
    s = (jnp.einsum('bthr,bsr->bhts', q_lat, c_kv).astype(jnp.float32)
         + jnp.einsum('bthp,bsp->bhts', q_pe, k_pe).astype(jnp.float32)) * ATTN_SCALE
    mask = (k_pos[None, :] // CHUNK) <= (q_pos[:, None] // CHUNK)
    s = jnp.where(mask[None, None], s, -jnp.inf)
    p = jax.nn.softmax(s, axis=-1).astype(c_kv.dtype)
    return jnp.einsum('bhts,bsr->bthr', p, c_kv)


def mla_mix(q_lat, q_pe, c_kv, k_pe, q_pos, k_pos):
    B, T, H, R = q_lat.shape
    if T > Q_BLOCK and T % Q_BLOCK == 0:
        nb = T // Q_BLOCK
        ql = q_lat.reshape(B, nb, Q_BLOCK, H, R).swapaxes(0, 1)
        qp = q_pe.reshape(B, nb, Q_BLOCK, H, QK_ROPE).swapaxes(0, 1)
        qpos = q_pos.reshape(nb, Q_BLOCK)
        o = lax.map(lambda a: mla_attend(a[0], a[1], c_kv, k_pe, a[2], k_pos), (ql, qp, qpos))
        return o.swapaxes(0, 1).reshape(B, T, H, R)
    return mla_attend(q_lat, q_pe, c_kv, k_pe, q_pos, k_pos)


def trunk_layer(x, q_pos, conv_prev, ffn_prev, ckv_past, kpe_past,
                w_in, w_conv, g_qa, w_uq, g_kva, w_uk, w_uv, w_o, g_mix_pre, g_mix_post,
                w_up, w_ffn_conv, b_ffn_conv, w_down, g_ffn_pre, g_ffn_post):
    B, T, _ = x.shape
    h = rmsnorm(x, g_mix_pre)
    z = h @ w_in
    xv, gb, gc, qa, kva, kpe = jnp.split(z, IN_SPLITS, axis=-1)
    conv_out, conv_new = causal_dwconv(gc * xv, conv_prev, w_conv)
    y_conv = gb * conv_out
    q = (rmsnorm(qa, g_qa) @ w_uq).reshape(B, T, N_HEADS, QK_NOPE + QK_ROPE)
    q_nope, q_pe = q[..., :QK_NOPE], q[..., QK_NOPE:]
    cos, sin = rope_tables(q_pos)
    q_pe = apply_rope(q_pe, cos[None, :, None], sin[None, :, None])
    ckv = rmsnorm(kva, g_kva)
    kpe = apply_rope(kpe, cos[None], sin[None])
    c_all = jnp.concatenate([ckv_past.astype(ckv.dtype), ckv], axis=1)
    k_all = jnp.concatenate([kpe_past.astype(kpe.dtype), kpe], axis=1)
    k_pos = jnp.arange(c_all.shape[1])
    q_lat = jnp.einsum('bthn,rhn->bthr', q_nope, w_uk)
    o_lat = mla_mix(q_lat, q_pe, c_all, k_all, q_pos, k_pos)
    y_mla = jnp.einsum('bthr,rhv->bthv', o_lat, w_uv).reshape(B, T, N_HEADS * V_HEAD)
    mix = jnp.concatenate([y_conv, y_mla], axis=-1) @ w_o
    x = x + rmsnorm(mix, g_mix_post)
    h = rmsnorm(x, g_ffn_pre)
    u, ffn_new = causal_dwconv(h @ w_up, ffn_prev, w_ffn_conv)
    u = u + b_ffn_conv
    a, b = u[..., :D_FF], u[..., D_FF:]
    x = x + rmsnorm((jax.nn.silu(a) * b) @ w_down, g_ffn_post)
    return x, conv_new, ffn_new, ckv, kpe


def setup_inputs(seed: int = 0) -> dict:
    key = jax.random.key(seed)
    ks = jax.random.split(key, 32)
    f32 = jnp.float32

    def w(k, shape, fan_in):
        return jax.random.normal(k, shape, f32) * (fan_in ** -0.5)

    def gain(k, n):
        return 1.0 + 0.05 * jax.random.normal(k, (DEPTH, n), f32)

    return {
        "x_prompt": jax.random.normal(ks[0], (BATCH, SEQ, D_MODEL), f32),
        "x_sample": jax.random.normal(ks[1], (DEC_BATCH, DEC_SEQ, D_MODEL), f32),
        "cache_ckv": jax.random.normal(ks[2], (DEPTH, DEC_BATCH, PAST_LEN, KV_RANK), f32),
        "cache_kpe": jax.random.normal(ks[3], (DEPTH, DEC_BATCH, PAST_LEN, QK_ROPE), f32),
        "state_conv": jax.random.normal(ks[4], (DEPTH, DEC_BATCH, CONV_W - 1, D_CONV), f32),
        "state_ffn": jax.random.normal(ks[5], (DEPTH, DEC_BATCH, CONV_W - 1, 2 * D_FF), f32),
        "w_in": w(ks[6], (DEPTH, D_MODEL, D_IN), D_MODEL),
        "w_conv": w(ks[7], (DEPTH, CONV_W, D_CONV), CONV_W),
        "g_qa": gain(ks[8], Q_RANK),
        "w_uq": w(ks[9], (DEPTH, Q_RANK, N_HEADS * (QK_NOPE + QK_ROPE)), Q_RANK),
        "g_kva": gain(ks[10], KV_RANK),
        "w_uk": w(ks[11], (DEPTH, KV_RANK, N_HEADS, QK_NOPE), KV_RANK),
        "w_uv": w(ks[12], (DEPTH, KV_RANK, N_HEADS, V_HEAD), KV_RANK),
        "w_o": w(ks[13], (DEPTH, D_MIX, D_MODEL), D_MIX),
        "g_mix_pre": gain(ks[14], D_MODEL),
        "g_mix_post": gain(ks[15], D_MODEL),
        "w_up": w(ks[16], (DEPTH, D_MODEL, 2 * D_FF), D_MODEL),
        "w_ffn_conv": w(ks[17], (DEPTH, CONV_W, 2 * D_FF), CONV_W),
        "b_ffn_conv": 0.01 * jax.random.normal(ks[18], (DEPTH, 2 * D_FF), f32),
        "w_down": w(ks[19], (DEPTH, D_FF, D_MODEL), D_FF),
        "g_ffn_pre": gain(ks[20], D_MODEL),
        "g_ffn_post": gain(ks[21], D_MODEL),
    }


def reference(x_prompt, x_sample, cache_ckv, cache_kpe, state_conv, state_ffn,
              w_in, w_conv, g_qa, w_uq, g_kva, w_uk, w_uv, w_o, g_mix_pre, g_mix_post,
              w_up, w_ffn_conv, b_ffn_conv, w_down, g_ffn_pre, g_ffn_post):
    Bp, Tp, _ = x_prompt.shape
    Bs, Ts, _ = x_sample.shape
    past_len = cache_ckv.shape[2]
    p_pos = jnp.arange(Tp)
    s_pos = past_len + jnp.arange(Ts)
    dt = x_prompt.dtype
    p_conv0 = jnp.zeros((Bp, CONV_W - 1, D_CONV), dt)
    p_ffn0 = jnp.zeros((Bp, CONV_W - 1, 2 * D_FF), dt)
    p_ckv0 = jnp.zeros((Bp, 0, KV_RANK), dt)
    p_kpe0 = jnp.zeros((Bp, 0, QK_ROPE), dt)

    xp, xs = x_prompt, x_sample
    pc, pk, pcv, pf = [], [], [], []
    sc, sk, scv, sf = [], [], [], []
    for l in range(DEPTH):
        wl = (w_in[l], w_conv[l], g_qa[l], w_uq[l], g_kva[l], w_uk[l], w_uv[l], w_o[l],
              g_mix_pre[l], g_mix_post[l], w_up[l], w_ffn_conv[l], b_ffn_conv[l], w_down[l],
              g_ffn_pre[l], g_ffn_post[l])
        xp, conv_p, ffn_p, ckv_p, kpe_p = trunk_layer(xp, p_pos, p_conv0, p_ffn0, p_ckv0, p_kpe0, *wl)
        xs, conv_s, ffn_s, ckv_s, kpe_s = trunk_layer(xs, s_pos, state_conv[l], state_ffn[l],
                                                      cache_ckv[l], cache_kpe[l], *wl)
        pc.append(ckv_p); pk.append(kpe_p); pcv.append(conv_p); pf.append(ffn_p)
        sc.append(ckv_s); sk.append(kpe_s); scv.append(conv_s); sf.append(ffn_s)

    p_ckv = jnp.stack(pc); p_kpe = jnp.stack(pk); p_conv = jnp.stack(pcv); p_ffn = jnp.stack(pf)
    s_ckv = jnp.stack(sc); s_kpe = jnp.stack(sk); s_conv = jnp.stack(scv); s_ffn = jnp.stack(sf)
    return (xp, xs, p_ckv, p_kpe, p_conv, p_ffn, s_ckv, s_kpe, s_conv, s_ffn)
```

```python
import functools
import math

import jax
import jax.numpy as jnp
from jax import lax
from jax.experimental import pallas as pl
from jax.experimental.pallas import tpu as pltpu

D_MODEL = 1024
CHUNK = 64
CONV_W = 3
D_CONV = D_MODEL // 2
N_HEADS = 8
QK_NOPE = 64
QK_ROPE = 32
V_HEAD = 64
KV_RANK = 128
Q_RANK = 256
D_FF = 2816
ROPE_BASE = 10000.0
EPS = 1e-6
ATTN_SCALE = 1.0 / math.sqrt(QK_NOPE + QK_ROPE)

D_QK = KV_RANK + QK_ROPE
D_IN_PAD = 2048
COL_QA = 3 * D_CONV
COL_KV = COL_QA + Q_RANK
FF_CHUNK = 256
N_FF_CHUNKS = D_FF // FF_CHUNK
HALO = 8
SAMPLE_SEQS_PER_STEP = 16

VMEM_LIMIT_BYTES = 56 * 1024 * 1024

BF16 = jnp.bfloat16
F32 = jnp.float32


def _rms(x, g):
    return x * lax.rsqrt(jnp.mean(x * x, axis=-1, keepdims=True) + EPS) * g


def _dot(a, b):
    return jnp.dot(a, b, preferred_element_type=F32)


def _dot_nt(a, b):
    return lax.dot_general(a, b, (((1,), (1,)), ((), ())), preferred_element_type=F32)


def _chunk_of(pos):
    return lax.shift_right_logical(pos, CHUNK.bit_length() - 1)


def _rope(x, rc, rs1, rs2):
    return x * rc + pltpu.roll(x, 128 - QK_ROPE // 2, 1) * rs1 + pltpu.roll(x, QK_ROPE // 2, 1) * rs2


def _causal_conv(buf_ref, cur, prev, w):
    tt = cur.shape[1]
    buf_ref[:, HALO - 2:HALO, :] = prev
    buf_ref[:, HALO:HALO + tt, :] = cur
    s1 = buf_ref[:, HALO - 1:HALO - 1 + tt, :]
    s2 = buf_ref[:, HALO - 2:HALO - 2 + tt, :]
    return s2 * w[0] + s1 * w[1] + cur * w[2]


def _mix_in_kernel(x_ref, cs_ref, w1_ref, wconv_ref, gpre_ref, gqa_ref, wuq_ref, gkva_ref, wukp_ref,
                   rc_ref, rs1_ref, rs2_ref,
                   yconv_ref, q_ref, kcat_ref, ckv_ref, kpe_ref, cso_ref,
                   cbuf_ref, *, nseq, tt):
    rows = nseq * tt

    @pl.when(pl.program_id(1) == 0)
    def _():
        cso_ref[...] = cs_ref[...]

    x = x_ref[...].reshape(rows, D_MODEL)
    h = _rms(x, gpre_ref[...]).astype(BF16)

    zc = _dot(h, w1_ref[:, 0:COL_QA])
    xv = zc[:, 0:D_CONV]
    gb = zc[:, D_CONV:2 * D_CONV]
    gc = zc[:, 2 * D_CONV:3 * D_CONV]
    c = (gc * xv).reshape(nseq, tt, D_CONV)
    conv = _causal_conv(cbuf_ref, c, cso_ref[...], wconv_ref[...])
    yconv_ref[...] = (gb.reshape(nseq, tt, D_CONV) * conv).astype(BF16)
    cso_ref[...] = c[:, tt - 2:tt, :]

    rc = rc_ref[...]
    rs1 = rs1_ref[...]
    rs2 = rs2_ref[...]

    qa = _dot(h, w1_ref[:, COL_QA:COL_KV])
    qn = _rms(qa, gqa_ref[...]).astype(BF16)
    q = _dot(qn, wuq_ref[...])
    n_nope = N_HEADS * QK_NOPE
    for p in range(N_HEADS // 2):
        ql = _dot(q[:, 128 * p:128 * (p + 1)].astype(BF16), wukp_ref[p]) * ATTN_SCALE
        for j in range(2):
            q_ref[:, 2 * p + j, :, 0:KV_RANK] = (
                ql[:, KV_RANK * j:KV_RANK * (j + 1)].reshape(nseq, tt, KV_RANK).astype(BF16))
    heads_per_tile = 128 // QK_ROPE
    for half in range(N_HEADS // heads_per_tile):
        qp = _rope(q[:, n_nope + 128 * half:n_nope + 128 * (half + 1)], rc, rs1, rs2) * ATTN_SCALE
        for j in range(heads_per_tile):
            q_ref[:, heads_per_tile * half + j, :, KV_RANK:D_QK] = (
                qp[:, QK_ROPE * j:QK_ROPE * (j + 1)].reshape(nseq, tt, QK_ROPE).astype(BF16))

    kv = _dot(h, w1_ref[:, COL_KV:D_IN_PAD])
    ckv = _rms(kv[:, 0:KV_RANK], gkva_ref[...])
    kpe = _rope(kv[:, KV_RANK:2 * KV_RANK], rc, rs1, rs2)[:, 0:QK_ROPE]
    ckv_ref[...] = ckv.reshape(nseq, tt, KV_RANK)
    kpe_ref[...] = kpe.reshape(nseq, tt, QK_ROPE)
    kcat_ref[:, :, 0:KV_RANK] = ckv.reshape(nseq, tt, KV_RANK).astype(BF16)
    kcat_ref[:, :, KV_RANK:D_QK] = kpe.reshape(nseq, tt, QK_ROPE).astype(BF16)


def _const_spec(shape):
    nd = len(shape)
    return pl.BlockSpec(shape, lambda *_: (0,) * nd, pipeline_mode=pl.Buffered(1))


def _mix_in(x, conv_state, lw, ropes, *, nseq, tt):
    B, T, _ = x.shape
    nb, nt = B // nseq, T // tt
    rows = nseq * tt
    rope_spec = pl.BlockSpec((rows, 128), lambda b, t: (t, 0))
    in_specs = [
        pl.BlockSpec((nseq, tt, D_MODEL), lambda b, t: (b, t, 0)),
        pl.BlockSpec((nseq, CONV_W - 1, D_CONV), lambda b, t: (b, 0, 0)),
        _const_spec((D_MODEL, D_IN_PAD)),
        _const_spec((CONV_W, D_CONV)),
        _const_spec((1, D_MODEL)),
        _const_spec((1, Q_RANK)),
        _const_spec((Q_RANK, N_HEADS * (QK_NOPE + QK_ROPE))),
        _const_spec((1, KV_RANK)),
        _const_spec((N_HEADS // 2, 128, 256)),
        rope_spec, rope_spec, rope_spec,
    ]
    out_shape = (
        jax.ShapeDtypeStruct((B, T, D_CONV), BF16),
        jax.ShapeDtypeStruct((B, N_HEADS, T, D_QK), BF16),
        jax.ShapeDtypeStruct((B, T, D_QK), BF16),
        jax.ShapeDtypeStruct((B, T, KV_RANK), F32),
        jax.ShapeDtypeStruct((B, T, QK_ROPE), F32),
        jax.ShapeDtypeStruct((B, CONV_W - 1, D_CONV), F32),
    )
    out_specs = (
        pl.BlockSpec((nseq, tt, D_CONV), lambda b, t: (b, t, 0)),
        pl.BlockSpec((nseq, N_HEADS, tt, D_QK), lambda b, t: (b, 0, t, 0)),
        pl.BlockSpec((nseq, tt, D_QK), lambda b, t: (b, t, 0)),
        pl.BlockSpec((nseq, tt, KV_RANK), lambda b, t: (b, t, 0)),
        pl.BlockSpec((nseq, tt, QK_ROPE), lambda b, t: (b, t, 0)),
        pl.BlockSpec((nseq, CONV_W - 1, D_CONV), lambda b, t: (b, 0, 0)),
    )
    return pl.pallas_call(
        functools.partial(_mix_in_kernel, nseq=nseq, tt=tt),
        grid=(nb, nt),
        in_specs=in_specs,
        out_specs=out_specs,
        out_shape=out_shape,
        scratch_shapes=[pltpu.VMEM((nseq, tt + HALO, D_CONV), F32)],
        compiler_params=pltpu.CompilerParams(
            dimension_semantics=("arbitrary", "arbitrary"), vmem_limit_bytes=VMEM_LIMIT_BYTES),
        name="mix_in",
    )(x, conv_state, lw["w1"], lw["w_conv"], lw["g_mix_pre"], lw["g_qa"], lw["w_uq"], lw["g_kva"],
      lw["w_ukp"], *ropes)


def _attend_prompt_kernel(q_ref, k_ref, o_ref, m_ref, l_ref, acc_ref, *, tq, tk):
    rows = N_HEADS * tq
    q0 = pl.program_id(1) * tq
    q = q_ref[0].reshape(rows, D_QK)

    m_ref[...] = jnp.full(m_ref.shape, -jnp.inf, F32)
    l_ref[...] = jnp.zeros(l_ref.shape, F32)
    acc_ref[...] = jnp.zeros(acc_ref.shape, F32)

    def step(j, masked):
        k0 = pl.multiple_of(j * tk, tk)
        k = k_ref[0, pl.ds(k0, tk), :]
        s = _dot_nt(q, k)
        if masked:
            q_pos = q0 + (lax.broadcasted_iota(jnp.int32, (rows, tk), 0) & (tq - 1))
            k_pos = k0 + lax.broadcasted_iota(jnp.int32, (rows, tk), 1)
            s = jnp.where(_chunk_of(k_pos) <= _chunk_of(q_pos), s, -jnp.inf)
        m_prev = m_ref[...]
        m_new = jnp.maximum(m_prev, jnp.max(s, axis=1, keepdims=True))
        alpha = jnp.exp(m_prev - m_new)
        p = jnp.exp(s - m_new)
        l_ref[...] = alpha * l_ref[...] + jnp.sum(p, axis=1, keepdims=True)
        acc_ref[...] = alpha * acc_ref[...] + _dot(p.astype(BF16), k[:, 0:KV_RANK])
        m_ref[...] = m_new

    n_full = (q0 + CHUNK) // tk
    n_all = (q0 + tq + tk - 1) // tk

    def full_body(j, carry):
        step(j, False)
        return carry

    def edge_body(j, carry):
        step(j, True)
        return carry

    lax.fori_loop(0, n_full, full_body, 0)
    lax.fori_loop(n_full, n_all, edge_body, 0)

    o = acc_ref[...] / l_ref[...]
    for h in range(N_HEADS):
        o_ref[0, :, KV_RANK * h:KV_RANK * (h + 1)] = o[tq * h:tq * (h + 1), :].astype(BF16)


def _attend_prompt(q, kcat, *, tq, tk):
    B, _, T, _ = q.shape
    assert tq % CHUNK == 0 and tq & (tq - 1) == 0 and T % tq == 0 and T % tk == 0
    rows = N_HEADS * tq
    return pl.pallas_call(
        functools.partial(_attend_prompt_kernel, tq=tq, tk=tk),
        grid=(B, T // tq),
        in_specs=[
            pl.BlockSpec((1, N_HEADS, tq, D_QK), lambda b, i: (b, 0, i, 0)),
            pl.BlockSpec((1, T, D_QK), lambda b, i: (b, 0, 0)),
        ],
        out_specs=pl.BlockSpec((1, tq, N_HEADS * KV_RANK), lambda b, i: (b, i, 0)),
        out_shape=jax.ShapeDtypeStruct((B, T, N_HEADS * KV_RANK), BF16),
        scratch_shapes=[
            pltpu.VMEM((rows, 1), F32),
            pltpu.VMEM((rows, 1), F32),
            pltpu.VMEM((rows, KV_RANK), F32),
        ],
        compiler_params=pltpu.CompilerParams(
            dimension_semantics=("arbitrary", "arbitrary"), vmem_limit_bytes=VMEM_LIMIT_BYTES),
        name="attend_prompt",
    )(q, kcat)


def _attend_sample_kernel(q_ref, ckv_ref, kpe_ref, kn_ref, o_ref, *, ts, past):
    rows = N_HEADS * ts
    q = q_ref[0].reshape(rows, D_QK)
    ck = ckv_ref[0, 0].astype(BF16)
    kp = kpe_ref[0, 0].astype(BF16)
    kn = kn_ref[0]
    s_old = _dot_nt(q[:, 0:KV_RANK], ck) + _dot_nt(q[:, KV_RANK:D_QK], kp)
    s_new = _dot_nt(q, kn)
    q_pos = past + (lax.broadcasted_iota(jnp.int32, (rows, ts), 0) & (ts - 1))
    k_pos = past + lax.broadcasted_iota(jnp.int32, (rows, ts), 1)
    s_new = jnp.where(_chunk_of(k_pos) <= _chunk_of(q_pos), s_new, -jnp.inf)
    m = jnp.maximum(jnp.max(s_old, axis=1, keepdims=True), jnp.max(s_new, axis=1, keepdims=True))
    p_old = jnp.exp(s_old - m)
    p_new = jnp.exp(s_new - m)
    l = jnp.sum(p_old, axis=1, keepdims=True) + jnp.sum(p_new, axis=1, keepdims=True)
    o = (_dot(p_old.astype(BF16), ck) + _dot(p_new.astype(BF16), kn[:, 0:KV_RANK])) / l
    for h in range(N_HEADS):
        o_ref[0, :, KV_RANK * h:KV_RANK * (h + 1)] = o[ts * h:ts * (h + 1), :].astype(BF16)


def _attend_sample(q, cache_ckv, cache_kpe, kcat, layer):
    B, _, ts, _ = q.shape
    past = cache_ckv.shape[2]
    return pl.pallas_call(
        functools.partial(_attend_sample_kernel, ts=ts, past=past),
        grid=(B,),
        in_specs=[
            pl.BlockSpec((1, N_HEADS, ts, D_QK), lambda b: (b, 0, 0, 0)),
            pl.BlockSpec((1, 1, past, KV_RANK), lambda b: (layer, b, 0, 0)),
            pl.BlockSpec((1, 1, past, QK_ROPE), lambda b: (layer, b, 0, 0)),
            pl.BlockSpec((1, ts, D_QK), lambda b: (b, 0, 0)),
        ],
        out_specs=pl.BlockSpec((1, ts, N_HEADS * KV_RANK), lambda b: (b, 0, 0)),
        out_shape=jax.ShapeDtypeStruct((B, ts, N_HEADS * KV_RANK), BF16),
        compiler_params=pltpu.CompilerParams(
            dimension_semantics=("arbitrary",), vmem_limit_bytes=VMEM_LIMIT_BYTES),
        name="attend_sample",
    )(q, cache_ckv, cache_kpe, kcat)


def _post_kernel(x_ref, yconv_ref, olat_ref, fs_ref, wuvg_ref, wo_ref, gpost_ref, gfpre_ref, wup_ref,
                 wfc_ref, bfc_ref, wdown_ref, gfpost_ref,
                 xo_ref, fso_ref,
                 ubuf_ref, acc_ref, hf_ref, *, nseq, tt):
    rows = nseq * tt

    @pl.when(pl.program_id(1) == 0)
    def _():
        fso_ref[...] = fs_ref[...]

    half = D_MODEL // 2
    ol = olat_ref[...].reshape(rows, N_HEADS * KV_RANK)
    mix = _dot(yconv_ref[...].reshape(rows, D_CONV), wo_ref[0:D_CONV, :])
    for g in range(2):
        ym = _dot(ol[:, half * g:half * (g + 1)], wuvg_ref[g]).astype(BF16)
        mix = mix + _dot(ym, wo_ref[D_CONV + 256 * g:D_CONV + 256 * (g + 1), :])
    x1 = x_ref[...].reshape(rows, D_MODEL) + _rms(mix, gpost_ref[...])
    xo_ref[...] = x1.reshape(nseq, tt, D_MODEL)
    hf_ref[...] = _rms(x1, gfpre_ref[...]).astype(BF16)
    acc_ref[...] = jnp.zeros(acc_ref.shape, F32)

    def chunk(c, carry):
        up = _dot(hf_ref[...], wup_ref[c]).reshape(nseq, tt, 2 * FF_CHUNK)
        u = _causal_conv(ubuf_ref, up, fso_ref[c], wfc_ref[c]) + bfc_ref[c]
        fso_ref[c] = up[:, tt - 2:tt, :]
        u = u.reshape(rows, 2 * FF_CHUNK)
        a = u[:, 0:FF_CHUNK]
        b = u[:, FF_CHUNK:2 * FF_CHUNK]
        gated = (a * (1.0 / (1.0 + jnp.exp(-a))) * b).astype(BF16)
        acc_ref[...] += _dot(gated, wdown_ref[c])
        return carry

    lax.fori_loop(0, N_FF_CHUNKS, chunk, 0)
    xo_ref[...] = xo_ref[...] + _rms(acc_ref[...], gfpost_ref[...]).reshape(nseq, tt, D_MODEL)


def _post(x, yconv, olat, ffn_state, lw, *, nseq, tt):
    B, T, _ = x.shape
    nb, nt = B // nseq, T // tt
    rows = nseq * tt
    state_block = (N_FF_CHUNKS, nseq, CONV_W - 1, 2 * FF_CHUNK)
    in_specs = [
        pl.BlockSpec((nseq, tt, D_MODEL), lambda b, t: (b, t, 0)),
        pl.BlockSpec((nseq, tt, D_CONV), lambda b, t: (b, t, 0)),
        pl.BlockSpec((nseq, tt, N_HEADS * KV_RANK), lambda b, t: (b, t, 0)),
        pl.BlockSpec(state_block, lambda b, t: (0, b, 0, 0), pipeline_mode=pl.Buffered(1)),
        _const_spec((2, 4 * KV_RANK, 4 * V_HEAD)),
        _const_spec((D_MODEL, D_MODEL)),
        _const_spec((1, D_MODEL)),
        _const_spec((1, D_MODEL)),
        _const_spec((N_FF_CHUNKS, D_MODEL, 2 * FF_CHUNK)),
        _const_spec((N_FF_CHUNKS, CONV_W, 2 * FF_CHUNK)),
        _const_spec((N_FF_CHUNKS, 1, 2 * FF_CHUNK)),
        _const_spec((N_FF_CHUNKS, FF_CHUNK, D_MODEL)),
        _const_spec((1, D_MODEL)),
    ]
    return pl.pallas_call(
        functools.partial(_post_kernel, nseq=nseq, tt=tt),
        grid=(nb, nt),
        in_specs=in_specs,
        out_specs=(
            pl.BlockSpec((nseq, tt, D_MODEL), lambda b, t: (b, t, 0)),
            pl.BlockSpec(state_block, lambda b, t: (0, b, 0, 0)),
        ),
        out_shape=(
            jax.ShapeDtypeStruct((B, T, D_MODEL), F32),
            jax.ShapeDtypeStruct((N_FF_CHUNKS, B, CONV_W - 1, 2 * FF_CHUNK), F32),
        ),
        scratch_shapes=[
            pltpu.VMEM((nseq, tt + HALO, 2 * FF_CHUNK), F32),
            pltpu.VMEM((rows, D_MODEL), F32),
            pltpu.VMEM((rows, D_MODEL), BF16),
        ],
        compiler_params=pltpu.CompilerParams(
            dimension_semantics=("arbitrary", "arbitrary"), vmem_limit_bytes=VMEM_LIMIT_BYTES),
        name="post_ffn",
    )(x, yconv, olat, ffn_state, lw["w_uvg"], lw["w_o"], lw["g_mix_post"], lw["g_ffn_pre"], lw["w_up"],
      lw["w_ffn_conv"], lw["b_ffn_conv"], lw["w_down"], lw["g_ffn_post"])


def _chunk_ff(a):
    lead = a.shape[:-1]
    a = a.reshape(lead + (2, N_FF_CHUNKS, FF_CHUNK))
    a = jnp.moveaxis(a, -2, 0)
    return a.reshape((N_FF_CHUNKS,) + lead + (2 * FF_CHUNK,))


def _unchunk_ff(a):
    lead = a.shape[1:-1]
    a = a.reshape((N_FF_CHUNKS,) + lead + (2, FF_CHUNK))
    a = jnp.moveaxis(a, 0, -2)
    return a.reshape(lead + (2 * D_FF,))


def _block_diag(blocks):
    n = len(blocks)
    rows = []
    for i, blk in enumerate(blocks):
        rows.append(jnp.concatenate(
            [blk if j == i else jnp.zeros((blk.shape[0], blocks[j].shape[1]), blk.dtype) for j in range(n)],
            axis=1))
    return jnp.concatenate(rows, axis=0)


def _layer_weights(l, w_in, w_conv, g_qa, w_uq, g_kva, w_uk, w_uv, w_o, g_mix_pre, g_mix_post,
                   w_up, w_ffn_conv, b_ffn_conv, w_down, g_ffn_pre, g_ffn_post):
    d_in = w_in.shape[2]
    uq = w_uq[l].reshape(Q_RANK, N_HEADS, QK_NOPE + QK_ROPE)
    uq = jnp.concatenate([uq[:, :, :QK_NOPE].reshape(Q_RANK, N_HEADS * QK_NOPE),
                          uq[:, :, QK_NOPE:].reshape(Q_RANK, N_HEADS * QK_ROPE)], axis=1)
    uk_t = jnp.transpose(w_uk[l], (1, 2, 0))
    uv = jnp.transpose(w_uv[l], (1, 0, 2))
    return {
        "w1": jnp.pad(w_in[l], ((0, 0), (0, D_IN_PAD - d_in))).astype(BF16),
        "w_conv": w_conv[l],
        "g_mix_pre": g_mix_pre[l][None],
        "g_qa": g_qa[l][None],
        "w_uq": uq.astype(BF16),
        "g_kva": g_kva[l][None],
        "w_ukp": jnp.stack([_block_diag([uk_t[2 * p], uk_t[2 * p + 1]])
                            for p in range(N_HEADS // 2)]).astype(BF16),
        "w_uvg": jnp.stack([_block_diag([uv[4 * g + j] for j in range(4)]) for g in range(2)]).astype(BF16),
        "w_o": w_o[l].astype(BF16),
        "g_mix_post": g_mix_post[l][None],
        "g_ffn_pre": g_ffn_pre[l][None],
        "w_up": _chunk_ff(w_up[l]).astype(BF16),
        "w_ffn_conv": _chunk_ff(w_ffn_conv[l]),
        "b_ffn_conv": _chunk_ff(b_ffn_conv[l][None]),
        "w_down": w_down[l].reshape(N_FF_CHUNKS, FF_CHUNK, D_MODEL).astype(BF16),
        "g_ffn_post": g_ffn_post[l][None],
    }


def _rope_tables(pos, reps):
    inv = 1.0 / (ROPE_BASE ** (jnp.arange(0, QK_ROPE, 2, dtype=F32) / QK_ROPE))
    ang = pos.astype(F32)[:, None] * inv[None, :]
    cos, sin = jnp.cos(ang), jnp.sin(ang)
    zero = jnp.zeros_like(sin)
    group = 128 // QK_ROPE
    rc = jnp.tile(jnp.concatenate([cos, cos], axis=1), (reps, group))
    rs1 = jnp.tile(jnp.concatenate([-sin, zero], axis=1), (reps, group))
    rs2 = jnp.tile(jnp.concatenate([zero, sin], axis=1), (reps, group))
    return rc, rs1, rs2


def _pick_tile(T, target):
    tt = min(T, target)
    assert T % tt == 0 and tt % 16 == 0
    return tt


def _trunk(x_prompt, x_sample, cache_ckv, cache_kpe, state_conv, state_ffn, weights):
    depth = cache_ckv.shape[0]
    Bp, Tp, _ = x_prompt.shape
    Bs, Ts, _ = x_sample.shape
    past = cache_ckv.shape[2]
    assert Ts >= CONV_W - 1 and Ts % 16 == 0

    tt = _pick_tile(Tp, 512)
    tq = _pick_tile(Tp, 128)
    tk = _pick_tile(Tp, 512)
    ns = math.gcd(Bs, SAMPLE_SEQS_PER_STEP)
    p_ropes = _rope_tables(jnp.arange(Tp), 1)
    s_ropes = _rope_tables(past + jnp.arange(Ts), ns)
    p_conv0 = jnp.zeros((Bp, CONV_W - 1, D_CONV), F32)
    p_ffn0 = jnp.zeros((N_FF_CHUNKS, Bp, CONV_W - 1, 2 * FF_CHUNK), F32)

    xp, xs = x_prompt, x_sample
    outs = [[] for _ in range(8)]
    for l in range(depth):
        lw = _layer_weights(l, *weights)
        yc_p, q_p, kcat_p, ckv_p, kpe_p, conv_p = _mix_in(xp, p_conv0, lw, p_ropes, nseq=1, tt=tt)
        yc_s, q_s, kcat_s, ckv_s, kpe_s, conv_s = _mix_in(xs, state_conv[l], lw, s_ropes, nseq=ns, tt=Ts)
        ol_p = _attend_prompt(q_p, kcat_p, tq=tq, tk=tk)
        ol_s = _attend_sample(q_s, cache_ckv, cache_kpe, kcat_s, l)
        xp, ffn_p = _post(xp, yc_p, ol_p, p_ffn0, lw, nseq=1, tt=tt)
        xs, ffn_s = _post(xs, yc_s, ol_s, _chunk_ff(state_ffn[l]), lw, nseq=ns, tt=Ts)
        for acc, v in zip(outs, (ckv_p, kpe_p, conv_p, _unchunk_ff(ffn_p),
                                 ckv_s, kpe_s, conv_s, _unchunk_ff(ffn_s))):
            acc.append(v)
    return (xp, xs) + tuple(jnp.stack(v) for v in outs)


def kernel(x_prompt, x_sample, cache_ckv, cache_kpe, state_conv, state_ffn, w_in, w_conv, g_qa, w_uq, g_kva, w_uk, w_uv, w_o, g_mix_pre, g_mix_post, w_up, w_ffn_conv, b_ffn_conv, w_down, g_ffn_pre, g_ffn_post):
    weights = (w_in, w_conv, g_qa, w_uq, g_kva, w_uk, w_uv, w_o, g_mix_pre, g_mix_post,
               w_up, w_ffn_conv, b_ffn_conv, w_down, g_ffn_pre, g_ffn_post)
    return _trunk(x_prompt, x_sample, cache_ckv, cache_kpe, state_conv, state_ffn, weights)
```

```python
import functools
import math

import jax
import jax.numpy as jnp
from jax import lax
from jax.experimental import pallas as pl
from jax.experimental.pallas import tpu as pltpu

D_MODEL = 1024
CHUNK = 64
CONV_W = 3
D_CONV = D_MODEL // 2
N_HEADS = 8
QK_NOPE = 64
QK_ROPE = 32
V_HEAD = 64
KV_RANK = 128
Q_RANK = 256
D_FF = 2816
ROPE_BASE = 10000.0
EPS = 1e-6
ATTN_SCALE = 1.0 / math.sqrt(QK_NOPE + QK_ROPE)
Q_SCALE = ATTN_SCALE * math.log2(math.e)

D_QK = KV_RANK + QK_ROPE
D_IN_PAD = 2048
COL_QA = 3 * D_CONV
COL_KV = COL_QA + Q_RANK
FF_CHUNK = 256
N_FF_CHUNKS = D_FF // FF_CHUNK
VT_ROWS = KV_RANK + 16
HALO = 8
SAMPLE_SEQS_PER_STEP = 16

VMEM_LIMIT_BYTES = 56 * 1024 * 1024

BF16 = jnp.bfloat16
F32 = jnp.float32


def _rms(x, g):
    return x * lax.rsqrt(jnp.mean(x * x, axis=-1, keepdims=True) + EPS) * g


def _dot(a, b):
    return jnp.dot(a, b, preferred_element_type=F32)


def _dot_nt(a, b):
    return lax.dot_general(a, b, (((1,), (1,)), ((), ())), preferred_element_type=F32)


def _chunk_of(pos):
    return lax.shift_right_logical(pos, CHUNK.bit_length() - 1)


def _rope(x, rc, rs1, rs2):
    return x * rc + pltpu.roll(x, 128 - QK_ROPE // 2, 1) * rs1 + pltpu.roll(x, QK_ROPE // 2, 1) * rs2


def _causal_conv(buf_ref, cur, prev, w):
    tt = cur.shape[1]
    buf_ref[:, HALO - 2:HALO, :] = prev
    buf_ref[:, HALO:HALO + tt, :] = cur
    s1 = buf_ref[:, HALO - 1:HALO - 1 + tt, :]
    s2 = buf_ref[:, HALO - 2:HALO - 2 + tt, :]
    return s2 * w[0] + s1 * w[1] + cur * w[2]


def _mix_in_kernel(x_ref, cs_ref, w1_ref, wconv_ref, gpre_ref, gqa_ref, wuq_ref, gkva_ref, wukp_ref,
                   rc_ref, rs1_ref, rs2_ref,
                   yconv_ref, q_ref, kcat_ref, ckv_ref, kpe_ref, cso_ref, *rest, nseq, tt, vt_tile):
    cbuf_ref = rest[-1]
    rows = nseq * tt

    @pl.when(pl.program_id(1) == 0)
    def _():
        cso_ref[...] = cs_ref[...]

    x = x_ref[...].reshape(rows, D_MODEL)
    h = _rms(x, gpre_ref[...]).astype(BF16)

    zc = _dot(h, w1_ref[:, 0:COL_QA])
    xv = zc[:, 0:D_CONV]
    gb = zc[:, D_CONV:2 * D_CONV]
    gc = zc[:, 2 * D_CONV:3 * D_CONV]
    c = (gc * xv).reshape(nseq, tt, D_CONV)
    conv = _causal_conv(cbuf_ref, c, cso_ref[...], wconv_ref[...])
    yconv_ref[...] = (gb.reshape(nseq, tt, D_CONV) * conv).astype(BF16)
    cso_ref[...] = c[:, tt - 2:tt, :]

    rc = rc_ref[...]
    rs1 = rs1_ref[...]
    rs2 = rs2_ref[...]

    qa = _dot(h, w1_ref[:, COL_QA:COL_KV])
    qn = _rms(qa, gqa_ref[...]).astype(BF16)
    q = _dot(qn, wuq_ref[...])
    n_nope = N_HEADS * QK_NOPE
    for p in range(N_HEADS // 2):
        ql = _dot(q[:, 128 * p:128 * (p + 1)].astype(BF16), wukp_ref[p]) * Q_SCALE
        for j in range(2):
            q_ref[:, 2 * p + j, :, 0:KV_RANK] = (
                ql[:, KV_RANK * j:KV_RANK * (j + 1)].reshape(nseq, tt, KV_RANK).astype(BF16))
    heads_per_tile = 128 // QK_ROPE
    for half in range(N_HEADS // heads_per_tile):
        qp = _rope(q[:, n_nope + 128 * half:n_nope + 128 * (half + 1)], rc, rs1, rs2) * Q_SCALE
        for j in range(heads_per_tile):
            q_ref[:, heads_per_tile * half + j, :, KV_RANK:D_QK] = (
                qp[:, QK_ROPE * j:QK_ROPE * (j + 1)].reshape(nseq, tt, QK_ROPE).astype(BF16))

    kv = _dot(h, w1_ref[:, COL_KV:D_IN_PAD])
    ckv = _rms(kv[:, 0:KV_RANK], gkva_ref[...])
    kpe = _rope(kv[:, KV_RANK:2 * KV_RANK], rc, rs1, rs2)[:, 0:QK_ROPE]
    ckv_ref[...] = ckv.reshape(nseq, tt, KV_RANK)
    kpe_ref[...] = kpe.reshape(nseq, tt, QK_ROPE)
    kcat_ref[:, :, 0:KV_RANK] = ckv.reshape(nseq, tt, KV_RANK).astype(BF16)
    kcat_ref[:, :, KV_RANK:D_QK] = kpe.reshape(nseq, tt, QK_ROPE).astype(BF16)
    if vt_tile:
        vt_ref = rest[0]
        for i in range(tt // vt_tile):
            vt_ref[0, i, 0:KV_RANK, :] = ckv[vt_tile * i:vt_tile * (i + 1), :].T.astype(BF16)
            vt_ref[0, i, KV_RANK:VT_ROWS, :] = jnp.ones((VT_ROWS - KV_RANK, vt_tile), BF16)


def _const_spec(shape):
    nd = len(shape)
    return pl.BlockSpec(shape, lambda *_: (0,) * nd, pipeline_mode=pl.Buffered(1))


def _mix_in(x, conv_state, lw, ropes, *, nseq, tt, vt_tile=0):
    B, T, _ = x.shape
    assert not vt_tile or (nseq == 1 and tt % vt_tile == 0)
    nb, nt = B // nseq, T // tt
    rows = nseq * tt
    rope_spec = pl.BlockSpec((rows, 128), lambda b, t: (t, 0))
    in_specs = [
        pl.BlockSpec((nseq, tt, D_MODEL), lambda b, t: (b, t, 0)),
        pl.BlockSpec((nseq, CONV_W - 1, D_CONV), lambda b, t: (b, 0, 0)),
        _const_spec((D_MODEL, D_IN_PAD)),
        _const_spec((CONV_W, D_CONV)),
        _const_spec((1, D_MODEL)),
        _const_spec((1, Q_RANK)),
        _const_spec((Q_RANK, N_HEADS * (QK_NOPE + QK_ROPE))),
        _const_spec((1, KV_RANK)),
        _const_spec((N_HEADS // 2, 128, 256)),
        rope_spec, rope_spec, rope_spec,
    ]
    out_shape = [
        jax.ShapeDtypeStruct((B, T, D_CONV), BF16),
        jax.ShapeDtypeStruct((B, N_HEADS, T, D_QK), BF16),
        jax.ShapeDtypeStruct((B, T, D_QK), BF16),
        jax.ShapeDtypeStruct((B, T, KV_RANK), F32),
        jax.ShapeDtypeStruct((B, T, QK_ROPE), F32),
        jax.ShapeDtypeStruct((B, CONV_W - 1, D_CONV), F32),
    ]
    out_specs = [
        pl.BlockSpec((nseq, tt, D_CONV), lambda b, t: (b, t, 0)),
        pl.BlockSpec((nseq, N_HEADS, tt, D_QK), lambda b, t: (b, 0, t, 0)),
        pl.BlockSpec((nseq, tt, D_QK), lambda b, t: (b, t, 0)),
        pl.BlockSpec((nseq, tt, KV_RANK), lambda b, t: (b, t, 0)),
        pl.BlockSpec((nseq, tt, QK_ROPE), lambda b, t: (b, t, 0)),
        pl.BlockSpec((nseq, CONV_W - 1, D_CONV), lambda b, t: (b, 0, 0)),
    ]
    if vt_tile:
        out_shape.append(jax.ShapeDtypeStruct((B, T // vt_tile, VT_ROWS, vt_tile), BF16))
        out_specs.append(pl.BlockSpec((1, tt // vt_tile, VT_ROWS, vt_tile), lambda b, t: (b, t, 0, 0)))
    return pl.pallas_call(
        functools.partial(_mix_in_kernel, nseq=nseq, tt=tt, vt_tile=vt_tile),
        grid=(nb, nt),
        in_specs=in_specs,
        out_specs=out_specs,
        out_shape=out_shape,
        scratch_shapes=[pltpu.VMEM((nseq, tt + HALO, D_CONV), F32)],
        compiler_params=pltpu.CompilerParams(
            dimension_semantics=("arbitrary", "arbitrary"), vmem_limit_bytes=VMEM_LIMIT_BYTES),
        name="mix_in",
    )(x, conv_state, lw["w1"], lw["w_conv"], lw["g_mix_pre"], lw["g_qa"], lw["w_uq"], lw["g_kva"],
      lw["w_ukp"], *ropes)


def _attend_prompt_kernel(q_ref, k_ref, vt_ref, o_ref, m_ref, acc_ref, *, tq, tk):
    cols = N_HEADS * tq
    q0 = pl.program_id(1) * tq
    q = q_ref[0].reshape(cols, D_QK)

    m_ref[...] = jnp.full(m_ref.shape, -jnp.inf, F32)
    acc_ref[...] = jnp.zeros(acc_ref.shape, F32)

    def step(j, masked):
        k0 = pl.multiple_of(j * tk, tk)
        s = _dot_nt(k_ref[0, pl.ds(k0, tk), :], q)
        if masked:
            k_pos = k0 + lax.broadcasted_iota(jnp.int32, (tk, cols), 0)
            q_pos = q0 + (lax.broadcasted_iota(jnp.int32, (tk, cols), 1) & (tq - 1))
            s = jnp.where(_chunk_of(k_pos) <= _chunk_of(q_pos), s, -jnp.inf)
        m_prev = m_ref[...]
        m_new = jnp.maximum(m_prev, jnp.max(s, axis=0, keepdims=True))
        p = jnp.exp2(s - m_new).astype(BF16)
        acc_ref[...] = jnp.exp2(m_prev - m_new) * acc_ref[...] + _dot(vt_ref[0, j], p)
        m_ref[...] = m_new

    n_full = (q0 + CHUNK) // tk
    n_all = (q0 + tq + tk - 1) // tk

    def full_body(j, carry):
        step(j, False)
        return carry

    def edge_body(j, carry):
        step(j, True)
        return carry

    lax.fori_loop(0, n_full, full_body, 0)
    lax.fori_loop(n_full, n_all, edge_body, 0)

    o = acc_ref[0:KV_RANK, :] / acc_ref[KV_RANK:KV_RANK + 1, :]
    for h in range(N_HEADS):
        o_ref[0, :, KV_RANK * h:KV_RANK * (h + 1)] = o[:, tq * h:tq * (h + 1)].T.astype(BF16)


def _attend_prompt(q, kcat, vt, *, tq):
    B, _, T, _ = q.shape
    tk = vt.shape[3]
    assert tq % CHUNK == 0 and tq & (tq - 1) == 0 and T % tq == 0 and T % tk == 0
    cols = N_HEADS * tq
    return pl.pallas_call(
        functools.partial(_attend_prompt_kernel, tq=tq, tk=tk),
        grid=(B, T // tq),
        in_specs=[
            pl.BlockSpec((1, N_HEADS, tq, D_QK), lambda b, i: (b, 0, i, 0)),
            pl.BlockSpec((1, T, D_QK), lambda b, i: (b, 0, 0)),
            pl.BlockSpec((1, T // tk, VT_ROWS, tk), lambda b, i: (b, 0, 0, 0)),
        ],
        out_specs=pl.BlockSpec((1, tq, N_HEADS * KV_RANK), lambda b, i: (b, i, 0)),
        out_shape=jax.ShapeDtypeStruct((B, T, N_HEADS * KV_RANK), BF16),
        scratch_shapes=[
            pltpu.VMEM((1, cols), F32),
            pltpu.VMEM((VT_ROWS, cols), F32),
        ],
        compiler_params=pltpu.CompilerParams(
            dimension_semantics=("arbitrary", "arbitrary"), vmem_limit_bytes=VMEM_LIMIT_BYTES),
        name="attend_prompt",
    )(q, kcat, vt)


def _attend_sample_kernel(q_ref, ckv_ref, kpe_ref, kn_ref, o_ref, *, ts, past):
    rows = N_HEADS * ts
    q = q_ref[0].reshape(rows, D_QK)
    ck = ckv_ref[0, 0].astype(BF16)
    kp = kpe_ref[0, 0].astype(BF16)
    kn = kn_ref[0]
    s_old = _dot_nt(q[:, 0:KV_RANK], ck) + _dot_nt(q[:, KV_RANK:D_QK], kp)
    s_new = _dot_nt(q, kn)
    q_pos = past + (lax.broadcasted_iota(jnp.int32, (rows, ts), 0) & (ts - 1))
    k_pos = past + lax.broadcasted_iota(jnp.int32, (rows, ts), 1)
    s_new = jnp.where(_chunk_of(k_pos) <= _chunk_of(q_pos), s_new, -jnp.inf)
    m = jnp.maximum(jnp.max(s_old, axis=1, keepdims=True), jnp.max(s_new, axis=1, keepdims=True))
    p_old = jnp.exp2(s_old - m)
    p_new = jnp.exp2(s_new - m)
    l = jnp.sum(p_old, axis=1, keepdims=True) + jnp.sum(p_new, axis=1, keepdims=True)
    o = (_dot(p_old.astype(BF16), ck) + _dot(p_new.astype(BF16), kn[:, 0:KV_RANK])) / l
    for h in range(N_HEADS):
        o_ref[0, :, KV_RANK * h:KV_RANK * (h + 1)] = o[ts * h:ts * (h + 1), :].astype(BF16)


def _attend_sample(q, cache_ckv, cache_kpe, kcat, layer):
    B, _, ts, _ = q.shape
    past = cache_ckv.shape[2]
    return pl.pallas_call(
        functools.partial(_attend_sample_kernel, ts=ts, past=past),
        grid=(B,),
        in_specs=[
            pl.BlockSpec((1, N_HEADS, ts, D_QK), lambda b: (b, 0, 0, 0)),
            pl.BlockSpec((1, 1, past, KV_RANK), lambda b: (layer, b, 0, 0)),
            pl.BlockSpec((1, 1, past, QK_ROPE), lambda b: (layer, b, 0, 0)),
            pl.BlockSpec((1, ts, D_QK), lambda b: (b, 0, 0)),
        ],
        out_specs=pl.BlockSpec((1, ts, N_HEADS * KV_RANK), lambda b: (b, 0, 0)),
        out_shape=jax.ShapeDtypeStruct((B, ts, N_HEADS * KV_RANK), BF16),
        compiler_params=pltpu.CompilerParams(
            dimension_semantics=("arbitrary",), vmem_limit_bytes=VMEM_LIMIT_BYTES),
        name="attend_sample",
    )(q, cache_ckv, cache_kpe, kcat)


def _post_kernel(x_ref, yconv_ref, olat_ref, fs_ref, wuvg_ref, wo_ref, gpost_ref, gfpre_ref, wup_ref,
                 wfc_ref, bfc_ref, wdown_ref, gfpost_ref,
                 xo_ref, fso_ref,
                 ubuf_ref, acc_ref, hf_ref, *, nseq, tt):
    rows = nseq * tt

    @pl.when(pl.program_id(1) == 0)
    def _():
        fso_ref[...] = fs_ref[...]

    half = D_MODEL // 2
    ol = olat_ref[...].reshape(rows, N_HEADS * KV_RANK)
    mix = _dot(yconv_ref[...].reshape(rows, D_CONV), wo_ref[0:D_CONV, :])
    for g in range(2):
        ym = _dot(ol[:, half * g:half * (g + 1)], wuvg_ref[g]).astype(BF16)
        mix = mix + _dot(ym, wo_ref[D_CONV + 256 * g:D_CONV + 256 * (g + 1), :])
    x1 = x_ref[...].reshape(rows, D_MODEL) + _rms(mix, gpost_ref[...])
    xo_ref[...] = x1.reshape(nseq, tt, D_MODEL)
    hf_ref[...] = _rms(x1, gfpre_ref[...]).astype(BF16)
    acc_ref[...] = jnp.zeros(acc_ref.shape, F32)

    def chunk(c, carry):
        up = _dot(hf_ref[...], wup_ref[c]).reshape(nseq, tt, 2 * FF_CHUNK)
        u = _causal_conv(ubuf_ref, up, fso_ref[c], wfc_ref[c]) + bfc_ref[c]
        fso_ref[c] = up[:, tt - 2:tt, :]
        u = u.reshape(rows, 2 * FF_CHUNK)
        a = u[:, 0:FF_CHUNK]
        b = u[:, FF_CHUNK:2 * FF_CHUNK]
        gated = (a * (1.0 / (1.0 + jnp.exp(-a))) * b).astype(BF16)
        acc_ref[...] += _dot(gated, wdown_ref[c])
        return carry

    lax.fori_loop(0, N_FF_CHUNKS, chunk, 0)
    xo_ref[...] = xo_ref[...] + _rms(acc_ref[...], gfpost_ref[...]).reshape(nseq, tt, D_MODEL)


def _post(x, yconv, olat, ffn_state, lw, *, nseq, tt):
    B, T, _ = x.shape
    nb, nt = B // nseq, T // tt
    rows = nseq * tt
    state_block = (N_FF_CHUNKS, nseq, CONV_W - 1, 2 * FF_CHUNK)
    in_specs = [
        pl.BlockSpec((nseq, tt, D_MODEL), lambda b, t: (b, t, 0)),
        pl.BlockSpec((nseq, tt, D_CONV), lambda b, t: (b, t, 0)),
        pl.BlockSpec((nseq, tt, N_HEADS * KV_RANK), lambda b, t: (b, t, 0)),
        pl.BlockSpec(state_block, lambda b, t: (0, b, 0, 0), pipeline_mode=pl.Buffered(1)),
        _const_spec((2, 4 * KV_RANK, 4 * V_HEAD)),
        _const_spec((D_MODEL, D_MODEL)),
        _const_spec((1, D_MODEL)),
        _const_spec((1, D_MODEL)),
        _const_spec((N_FF_CHUNKS, D_MODEL, 2 * FF_CHUNK)),
        _const_spec((N_FF_CHUNKS, CONV_W, 2 * FF_CHUNK)),
        _const_spec((N_FF_CHUNKS, 1, 2 * FF_CHUNK)),
        _const_spec((N_FF_CHUNKS, FF_CHUNK, D_MODEL)),
        _const_spec((1, D_MODEL)),
    ]
    return pl.pallas_call(
        functools.partial(_post_kernel, nseq=nseq, tt=tt),
        grid=(nb, nt),
        in_specs=in_specs,
        out_specs=(
            pl.BlockSpec((nseq, tt, D_MODEL), lambda b, t: (b, t, 0)),
            pl.BlockSpec(state_block, lambda b, t: (0, b, 0, 0)),
        ),
        out_shape=(
            jax.ShapeDtypeStruct((B, T, D_MODEL), F32),
            jax.ShapeDtypeStruct((N_FF_CHUNKS, B, CONV_W - 1, 2 * FF_CHUNK), F32),
        ),
        scratch_shapes=[
            pltpu.VMEM((nseq, tt + HALO, 2 * FF_CHUNK), F32),
            pltpu.VMEM((rows, D_MODEL), F32),
            pltpu.VMEM((rows, D_MODEL), BF16),
        ],
        compiler_params=pltpu.CompilerParams(
            dimension_semantics=("arbitrary", "arbitrary"), vmem_limit_bytes=VMEM_LIMIT_BYTES),
        name="post_ffn",
    )(x, yconv, olat, ffn_state, lw["w_uvg"], lw["w_o"], lw["g_mix_post"], lw["g_ffn_pre"], lw["w_up"],
      lw["w_ffn_conv"], lw["b_ffn_conv"], lw["w_down"], lw["g_ffn_post"])


def _chunk_ff(a):
    lead = a.shape[:-1]
    a = a.reshape(lead + (2, N_FF_CHUNKS, FF_CHUNK))
    a = jnp.moveaxis(a, -2, 0)
    return a.reshape((N_FF_CHUNKS,) + lead + (2 * FF_CHUNK,))


def _unchunk_ff(a):
    lead = a.shape[1:-1]
    a = a.reshape((N_FF_CHUNKS,) + lead + (2, FF_CHUNK))
    a = jnp.moveaxis(a, 0, -2)
    return a.reshape(lead + (2 * D_FF,))


def _block_diag(blocks):
    n = len(blocks)
    rows = []
    for i, blk in enumerate(blocks):
        rows.append(jnp.concatenate(
            [blk if j == i else jnp.zeros((blk.shape[0], blocks[j].shape[1]), blk.dtype) for j in range(n)],
            axis=1))
    return jnp.concatenate(rows, axis=0)


def _layer_weights(l, w_in, w_conv, g_qa, w_uq, g_kva, w_uk, w_uv, w_o, g_mix_pre, g_mix_post,
                   w_up, w_ffn_conv, b_ffn_conv, w_down, g_ffn_pre, g_ffn_post):
    d_in = w_in.shape[2]
    uq = w_uq[l].reshape(Q_RANK, N_HEADS, QK_NOPE + QK_ROPE)
    uq = jnp.concatenate([uq[:, :, :QK_NOPE].reshape(Q_RANK, N_HEADS * QK_NOPE),
                          uq[:, :, QK_NOPE:].reshape(Q_RANK, N_HEADS * QK_ROPE)], axis=1)
    uk_t = jnp.transpose(w_uk[l], (1, 2, 0))
    uv = jnp.transpose(w_uv[l], (1, 0, 2))
    return {
        "w1": jnp.pad(w_in[l], ((0, 0), (0, D_IN_PAD - d_in))).astype(BF16),
        "w_conv": w_conv[l],
        "g_mix_pre": g_mix_pre[l][None],
        "g_qa": g_qa[l][None],
        "w_uq": uq.astype(BF16),
        "g_kva": g_kva[l][None],
        "w_ukp": jnp.stack([_block_diag([uk_t[2 * p], uk_t[2 * p + 1]])
                            for p in range(N_HEADS // 2)]).astype(BF16),
        "w_uvg": jnp.stack([_block_diag([uv[4 * g + j] for j in range(4)]) for g in range(2)]).astype(BF16),
        "w_o": w_o[l].astype(BF16),
        "g_mix_post": g_mix_post[l][None],
        "g_ffn_pre": g_ffn_pre[l][None],
        "w_up": _chunk_ff(w_up[l]).astype(BF16),
        "w_ffn_conv": _chunk_ff(w_ffn_conv[l]),
        "b_ffn_conv": _chunk_ff(b_ffn_conv[l][None]),
        "w_down": w_down[l].reshape(N_FF_CHUNKS, FF_CHUNK, D_MODEL).astype(BF16),
        "g_ffn_post": g_ffn_post[l][None],
    }


def _rope_tables(pos, reps):
    inv = 1.0 / (ROPE_BASE ** (jnp.arange(0, QK_ROPE, 2, dtype=F32) / QK_ROPE))
    ang = pos.astype(F32)[:, None] * inv[None, :]
    cos, sin = jnp.cos(ang), jnp.sin(ang)
    zero = jnp.zeros_like(sin)
    group = 128 // QK_ROPE
    rc = jnp.tile(jnp.concatenate([cos, cos], axis=1), (reps, group))
    rs1 = jnp.tile(jnp.concatenate([-sin, zero], axis=1), (reps, group))
    rs2 = jnp.tile(jnp.concatenate([zero, sin], axis=1), (reps, group))
    return rc, rs1, rs2


def _pick_tile(T, target):
    tt = min(T, target)
    assert T % tt == 0 and tt % 16 == 0
    return tt


def _trunk(x_prompt, x_sample, cache_ckv, cache_kpe, state_conv, state_ffn, weights):
    depth = cache_ckv.shape[0]
    Bp, Tp, _ = x_prompt.shape
    Bs, Ts, _ = x_sample.shape
    past = cache_ckv.shape[2]
    assert Ts >= CONV_W - 1 and Ts % 16 == 0

    tt = _pick_tile(Tp, 512)
    tq = _pick_tile(Tp, 128)
    tk = _pick_tile(Tp, 512)
    ns = math.gcd(Bs, SAMPLE_SEQS_PER_STEP)
    p_ropes = _rope_tables(jnp.arange(Tp), 1)
    s_ropes = _rope_tables(past + jnp.arange(Ts), ns)
    p_conv0 = jnp.zeros((Bp, CONV_W - 1, D_CONV), F32)
    p_ffn0 = jnp.zeros((N_FF_CHUNKS, Bp, CONV_W - 1, 2 * FF_CHUNK), F32)

    xp, xs = x_prompt, x_sample
    outs = [[] for _ in range(8)]
    for l in range(depth):
        lw = _layer_weights(l, *weights)
        yc_p, q_p, kcat_p, ckv_p, kpe_p, conv_p, vt_p = _mix_in(
            xp, p_conv0, lw, p_ropes, nseq=1, tt=tt, vt_tile=tk)
        yc_s, q_s, kcat_s, ckv_s, kpe_s, conv_s = _mix_in(xs, state_conv[l], lw, s_ropes, nseq=ns, tt=Ts)
        ol_p = _attend_prompt(q_p, kcat_p, vt_p, tq=tq)
        ol_s = _attend_sample(q_s, cache_ckv, cache_kpe, kcat_s, l)
        xp, ffn_p = _post(xp, yc_p, ol_p, p_ffn0, lw, nseq=1, tt=tt)
        xs, ffn_s = _post(xs, yc_s, ol_s, _chunk_ff(state_ffn[l]), lw, nseq=ns, tt=Ts)
        for acc, v in zip(outs, (ckv_p, kpe_p, conv_p, _unchunk_ff(ffn_p),
                                 ckv_s, kpe_s, conv_s, _unchunk_ff(ffn_s))):
            acc.append(v)
    return (xp, xs) + tuple(jnp.stack(v) for v in outs)


def kernel(x_prompt, x_sample, cache_ckv, cache_kpe, state_conv, state_ffn, w_in, w_conv, g_qa, w_uq, g_kva, w_uk, w_uv, w_o, g_mix_pre, g_mix_post, w_up, w_ffn_conv, b_ffn_conv, w_down, g_ffn_pre, g_ffn_post):
    weights = (w_in, w_conv, g_qa, w_uq, g_kva, w_uk, w_uv, w_o, g_mix_pre, g_mix_post,
               w_up, w_ffn_conv, b_ffn_conv, w_down, g_ffn_pre, g_ffn_post)
    return _trunk(x_prompt, x_sample, cache_ckv, cache_kpe, state_conv, state_ffn, weights)
```

```python
import functools
import math

import jax
import jax.numpy as jnp
from jax import lax
from jax.experimental import pallas as pl
from jax.experimental.pallas import tpu as pltpu

D_MODEL = 1024
CHUNK = 64
CONV_W = 3
D_CONV = D_MODEL // 2
N_HEADS = 8
QK_NOPE = 64
QK_ROPE = 32
V_HEAD = 64
KV_RANK = 128
Q_RANK = 256
D_FF = 2816
ROPE_BASE = 10000.0
EPS = 1e-6
ATTN_SCALE = 1.0 / math.sqrt(QK_NOPE + QK_ROPE)
Q_SCALE = ATTN_SCALE * math.log2(math.e)

D_QK = KV_RANK + QK_ROPE
D_IN_PAD = 2048
COL_QA = 3 * D_CONV
COL_KV = COL_QA + Q_RANK
FF_CHUNK = 256
N_FF_CHUNKS = D_FF // FF_CHUNK
assert N_FF_CHUNKS % 2 == 1
ATTN_COL_GROUPS = 1
VT_ROWS = KV_RANK + 16
HALO = 8
SAMPLE_SEQS_PER_STEP = 16

VMEM_LIMIT_BYTES = 56 * 1024 * 1024

BF16 = jnp.bfloat16
F32 = jnp.float32


def _rms(x, g):
    return x * lax.rsqrt(jnp.mean(x * x, axis=-1, keepdims=True) + EPS) * g


def _dot(a, b):
    return jnp.dot(a, b, preferred_element_type=F32)


def _dot_nt(a, b):
    return lax.dot_general(a, b, (((1,), (1,)), ((), ())), preferred_element_type=F32)


def _chunk_of(pos):
    return lax.shift_right_logical(pos, CHUNK.bit_length() - 1)


def _rope(x, rc, rs1, rs2):
    return x * rc + pltpu.roll(x, 128 - QK_ROPE // 2, 1) * rs1 + pltpu.roll(x, QK_ROPE // 2, 1) * rs2


def _causal_conv(buf_ref, cur, prev, w):
    tt = cur.shape[1]
    buf_ref[:, HALO - 2:HALO, :] = prev
    buf_ref[:, HALO:HALO + tt, :] = cur
    s1 = buf_ref[:, HALO - 1:HALO - 1 + tt, :]
    s2 = buf_ref[:, HALO - 2:HALO - 2 + tt, :]
    return s2 * w[0] + s1 * w[1] + cur * w[2]


def _mix_in_kernel(x_ref, cs_ref, w1_ref, wconv_ref, gpre_ref, gqa_ref, wuq_ref, gkva_ref, wukp_ref,
                   rc_ref, rs1_ref, rs2_ref,
                   yconv_ref, q_ref, kcat_ref, ckv_ref, kpe_ref, cso_ref, *rest, nseq, tt, vt_tile, qt_tile):
    cbuf_ref = rest[-1]
    rows = nseq * tt

    @pl.when(pl.program_id(1) == 0)
    def _():
        cso_ref[...] = cs_ref[...]

    x = x_ref[...].reshape(rows, D_MODEL)
    h = _rms(x, gpre_ref[...]).astype(BF16)

    zc = _dot(h, w1_ref[:, 0:COL_QA])
    xv = zc[:, 0:D_CONV]
    gb = zc[:, D_CONV:2 * D_CONV]
    gc = zc[:, 2 * D_CONV:3 * D_CONV]
    c = (gc * xv).reshape(nseq, tt, D_CONV)
    conv = _causal_conv(cbuf_ref, c, cso_ref[...], wconv_ref[...])
    yconv_ref[...] = (gb.reshape(nseq, tt, D_CONV) * conv).astype(BF16)
    cso_ref[...] = c[:, tt - 2:tt, :]

    rc = rc_ref[...]
    rs1 = rs1_ref[...]
    rs2 = rs2_ref[...]

    qa = _dot(h, w1_ref[:, COL_QA:COL_KV])
    qn = _rms(qa, gqa_ref[...]).astype(BF16)
    q = _dot(qn, wuq_ref[...])
    n_nope = N_HEADS * QK_NOPE
    heads_per_tile = 128 // QK_ROPE
    for p in range(N_HEADS // 2):
        ql = _dot(q[:, 128 * p:128 * (p + 1)].astype(BF16), wukp_ref[p]) * Q_SCALE
        for j in range(2):
            hd = 2 * p + j
            qh = ql[:, KV_RANK * j:KV_RANK * (j + 1)]
            if qt_tile:
                qh_t = qh.T.astype(BF16)
                for i in range(tt // qt_tile):
                    q_ref[0, i, 0:KV_RANK, qt_tile * hd:qt_tile * (hd + 1)] = (
                        qh_t[:, qt_tile * i:qt_tile * (i + 1)])
            else:
                q_ref[:, hd, :, 0:KV_RANK] = qh.reshape(nseq, tt, KV_RANK).astype(BF16)
    for half in range(N_HEADS // heads_per_tile):
        qp = _rope(q[:, n_nope + 128 * half:n_nope + 128 * (half + 1)], rc, rs1, rs2) * Q_SCALE
        qp_t = qp.T.astype(BF16) if qt_tile else None
        for j in range(heads_per_tile):
            hd = heads_per_tile * half + j
            if qt_tile:
                for i in range(tt // qt_tile):
                    q_ref[0, i, KV_RANK:D_QK, qt_tile * hd:qt_tile * (hd + 1)] = (
                        qp_t[QK_ROPE * j:QK_ROPE * (j + 1), qt_tile * i:qt_tile * (i + 1)])
            else:
                q_ref[:, hd, :, KV_RANK:D_QK] = (
                    qp[:, QK_ROPE * j:QK_ROPE * (j + 1)].reshape(nseq, tt, QK_ROPE).astype(BF16))

    kv = _dot(h, w1_ref[:, COL_KV:D_IN_PAD])
    ckv = _rms(kv[:, 0:KV_RANK], gkva_ref[...])
    kpe = _rope(kv[:, KV_RANK:2 * KV_RANK], rc, rs1, rs2)[:, 0:QK_ROPE]
    ckv_ref[...] = ckv.reshape(nseq, tt, KV_RANK)
    kpe_ref[...] = kpe.reshape(nseq, tt, QK_ROPE)
    kcat_ref[:, :, 0:KV_RANK] = ckv.reshape(nseq, tt, KV_RANK).astype(BF16)
    kcat_ref[:, :, KV_RANK:D_QK] = kpe.reshape(nseq, tt, QK_ROPE).astype(BF16)
    if vt_tile:
        vt_ref = rest[0]
        for i in range(tt // vt_tile):
            vt_ref[0, i, 0:KV_RANK, :] = ckv[vt_tile * i:vt_tile * (i + 1), :].T.astype(BF16)
            vt_ref[0, i, KV_RANK:VT_ROWS, :] = jnp.ones((VT_ROWS - KV_RANK, vt_tile), BF16)


def _const_spec(shape):
    nd = len(shape)
    return pl.BlockSpec(shape, lambda *_: (0,) * nd, pipeline_mode=pl.Buffered(1))


def _mix_in(x, conv_state, lw, ropes, *, nseq, tt, vt_tile=0, qt_tile=0):
    B, T, _ = x.shape
    assert not vt_tile or (nseq == 1 and tt % vt_tile == 0)
    assert not qt_tile or (nseq == 1 and tt % qt_tile == 0)
    if qt_tile:
        q_shape = jax.ShapeDtypeStruct((B, T // qt_tile, D_QK, N_HEADS * qt_tile), BF16)
        q_spec = pl.BlockSpec((1, tt // qt_tile, D_QK, N_HEADS * qt_tile), lambda b, t: (b, t, 0, 0))
    else:
        q_shape = jax.ShapeDtypeStruct((B, N_HEADS, T, D_QK), BF16)
        q_spec = pl.BlockSpec((nseq, N_HEADS, tt, D_QK), lambda b, t: (b, 0, t, 0))
    nb, nt = B // nseq, T // tt
    rows = nseq * tt
    rope_spec = pl.BlockSpec((rows, 128), lambda b, t: (t, 0))
    in_specs = [
        pl.BlockSpec((nseq, tt, D_MODEL), lambda b, t: (b, t, 0)),
        pl.BlockSpec((nseq, CONV_W - 1, D_CONV), lambda b, t: (b, 0, 0)),
        _const_spec((D_MODEL, D_IN_PAD)),
        _const_spec((CONV_W, D_CONV)),
        _const_spec((1, D_MODEL)),
        _const_spec((1, Q_RANK)),
        _const_spec((Q_RANK, N_HEADS * (QK_NOPE + QK_ROPE))),
        _const_spec((1, KV_RANK)),
        _const_spec((N_HEADS // 2, 128, 256)),
        rope_spec, rope_spec, rope_spec,
    ]
    out_shape = [
        jax.ShapeDtypeStruct((B, T, D_CONV), BF16),
        q_shape,
        jax.ShapeDtypeStruct((B, T, D_QK), BF16),
        jax.ShapeDtypeStruct((B, T, KV_RANK), F32),
        jax.ShapeDtypeStruct((B, T, QK_ROPE), F32),
        jax.ShapeDtypeStruct((B, CONV_W - 1, D_CONV), F32),
    ]
    out_specs = [
        pl.BlockSpec((nseq, tt, D_CONV), lambda b, t: (b, t, 0)),
        q_spec,
        pl.BlockSpec((nseq, tt, D_QK), lambda b, t: (b, t, 0)),
        pl.BlockSpec((nseq, tt, KV_RANK), lambda b, t: (b, t, 0)),
        pl.BlockSpec((nseq, tt, QK_ROPE), lambda b, t: (b, t, 0)),
        pl.BlockSpec((nseq, CONV_W - 1, D_CONV), lambda b, t: (b, 0, 0)),
    ]
    if vt_tile:
        out_shape.append(jax.ShapeDtypeStruct((B, T // vt_tile, VT_ROWS, vt_tile), BF16))
        out_specs.append(pl.BlockSpec((1, tt // vt_tile, VT_ROWS, vt_tile), lambda b, t: (b, t, 0, 0)))
    return pl.pallas_call(
        functools.partial(_mix_in_kernel, nseq=nseq, tt=tt, vt_tile=vt_tile, qt_tile=qt_tile),
        grid=(nb, nt),
        in_specs=in_specs,
        out_specs=out_specs,
        out_shape=out_shape,
        scratch_shapes=[pltpu.VMEM((nseq, tt + HALO, D_CONV), F32)],
        compiler_params=pltpu.CompilerParams(
            dimension_semantics=("arbitrary", "arbitrary"), vmem_limit_bytes=VMEM_LIMIT_BYTES),
        name="mix_in",
    )(x, conv_state, lw["w1"], lw["w_conv"], lw["g_mix_pre"], lw["g_qa"], lw["w_uq"], lw["g_kva"],
      lw["w_ukp"], *ropes)


def _attend_prompt_kernel(q_ref, k_ref, vt_ref, o_ref, m_ref, acc_ref, sa_ref, sb_ref, *, tq, tk):
    cols = N_HEADS * tq
    gw = cols // ATTN_COL_GROUPS
    q0 = pl.program_id(1) * tq
    n_last = q0 // tk

    m_ref[...] = jnp.full(m_ref.shape, -jnp.inf, F32)
    acc_ref[...] = jnp.zeros(acc_ref.shape, F32)

    def scores(j, s_ref):
        k0 = pl.multiple_of(j * tk, tk)
        s_ref[...] = _dot(k_ref[0, pl.ds(k0, tk), :], q_ref[0, 0])

    def consume(j, s_ref, masked):
        vt = vt_ref[0, j]
        for g in range(ATTN_COL_GROUPS):
            sl = slice(gw * g, gw * (g + 1))
            s = s_ref[:, sl]
            if masked:
                k_pos = j * tk + lax.broadcasted_iota(jnp.int32, (tk, gw), 0)
                q_pos = q0 + (lax.broadcasted_iota(jnp.int32, (tk, gw), 1) & (tq - 1))
                s = jnp.where(_chunk_of(k_pos) <= _chunk_of(q_pos), s, -jnp.inf)
            m_prev = m_ref[:, sl]
            m_new = jnp.maximum(m_prev, jnp.max(s, axis=0, keepdims=True))
            p = jnp.exp2(s - m_new).astype(BF16)
            acc_ref[:, sl] = jnp.exp2(m_prev - m_new) * acc_ref[:, sl] + _dot(vt, p)
            m_ref[:, sl] = m_new

    scores(0, sa_ref)

    def pair_body(i, carry):
        scores(2 * i + 1, sb_ref)
        consume(2 * i, sa_ref, False)
        scores(2 * i + 2, sa_ref)
        consume(2 * i + 1, sb_ref, False)
        return carry

    lax.fori_loop(0, n_last // 2, pair_body, 0)

    @pl.when(n_last % 2 == 1)
    def _():
        scores(n_last, sb_ref)
        consume(n_last - 1, sa_ref, False)
        consume(n_last, sb_ref, True)

    @pl.when(n_last % 2 == 0)
    def _():
        consume(n_last, sa_ref, True)

    o = acc_ref[0:KV_RANK, :] / acc_ref[KV_RANK:KV_RANK + 1, :]
    for h in range(N_HEADS):
        o_ref[0, :, KV_RANK * h:KV_RANK * (h + 1)] = o[:, tq * h:tq * (h + 1)].T.astype(BF16)


def _attend_prompt(q_t, kcat, vt):
    B, T, _ = kcat.shape
    tk = vt.shape[3]
    cols = q_t.shape[3]
    tq = cols // N_HEADS
    assert tq % CHUNK == 0 and tq & (tq - 1) == 0 and T % tk == 0 and tk % tq == 0
    return pl.pallas_call(
        functools.partial(_attend_prompt_kernel, tq=tq, tk=tk),
        grid=(B, T // tq),
        in_specs=[
            pl.BlockSpec((1, 1, D_QK, cols), lambda b, i: (b, i, 0, 0)),
            pl.BlockSpec((1, T, D_QK), lambda b, i: (b, 0, 0)),
            pl.BlockSpec((1, T // tk, VT_ROWS, tk), lambda b, i: (b, 0, 0, 0)),
        ],
        out_specs=pl.BlockSpec((1, tq, N_HEADS * KV_RANK), lambda b, i: (b, i, 0)),
        out_shape=jax.ShapeDtypeStruct((B, T, N_HEADS * KV_RANK), BF16),
        scratch_shapes=[
            pltpu.VMEM((1, cols), F32),
            pltpu.VMEM((VT_ROWS, cols), F32),
            pltpu.VMEM((tk, cols), F32),
            pltpu.VMEM((tk, cols), F32),
        ],
        compiler_params=pltpu.CompilerParams(
            dimension_semantics=("arbitrary", "arbitrary"), vmem_limit_bytes=VMEM_LIMIT_BYTES),
        name="attend_prompt",
    )(q_t, kcat, vt)


def _attend_sample_kernel(q_ref, ckv_ref, kpe_ref, kn_ref, o_ref, *, ts, past):
    rows = N_HEADS * ts
    q = q_ref[0].reshape(rows, D_QK)
    ck = ckv_ref[0, 0].astype(BF16)
    kp = kpe_ref[0, 0].astype(BF16)
    kn = kn_ref[0]
    s_old = _dot_nt(q[:, 0:KV_RANK], ck) + _dot_nt(q[:, KV_RANK:D_QK], kp)
    s_new = _dot_nt(q, kn)
    q_pos = past + (lax.broadcasted_iota(jnp.int32, (rows, ts), 0) & (ts - 1))
    k_pos = past + lax.broadcasted_iota(jnp.int32, (rows, ts), 1)
    s_new = jnp.where(_chunk_of(k_pos) <= _chunk_of(q_pos), s_new, -jnp.inf)
    m = jnp.maximum(jnp.max(s_old, axis=1, keepdims=True), jnp.max(s_new, axis=1, keepdims=True))
    p_old = jnp.exp2(s_old - m)
    p_new = jnp.exp2(s_new - m)
    l = jnp.sum(p_old, axis=1, keepdims=True) + jnp.sum(p_new, axis=1, keepdims=True)
    o = (_dot(p_old.astype(BF16), ck) + _dot(p_new.astype(BF16), kn[:, 0:KV_RANK])) / l
    for h in range(N_HEADS):
        o_ref[0, :, KV_RANK * h:KV_RANK * (h + 1)] = o[ts * h:ts * (h + 1), :].astype(BF16)


def _attend_sample(q, cache_ckv, cache_kpe, kcat, layer):
    B, _, ts, _ = q.shape
    past = cache_ckv.shape[2]
    return pl.pallas_call(
        functools.partial(_attend_sample_kernel, ts=ts, past=past),
        grid=(B,),
        in_specs=[
            pl.BlockSpec((1, N_HEADS, ts, D_QK), lambda b: (b, 0, 0, 0)),
            pl.BlockSpec((1, 1, past, KV_RANK), lambda b: (layer, b, 0, 0)),
            pl.BlockSpec((1, 1, past, QK_ROPE), lambda b: (layer, b, 0, 0)),
            pl.BlockSpec((1, ts, D_QK), lambda b: (b, 0, 0)),
        ],
        out_specs=pl.BlockSpec((1, ts, N_HEADS * KV_RANK), lambda b: (b, 0, 0)),
        out_shape=jax.ShapeDtypeStruct((B, ts, N_HEADS * KV_RANK), BF16),
        compiler_params=pltpu.CompilerParams(
            dimension_semantics=("arbitrary",), vmem_limit_bytes=VMEM_LIMIT_BYTES),
        name="attend_sample",
    )(q, cache_ckv, cache_kpe, kcat)


def _post_kernel(x_ref, yconv_ref, olat_ref, fs_ref, wuvg_ref, wo_ref, gpost_ref, gfpre_ref, wup_ref,
                 wfc_ref, bfc_ref, wdown_ref, gfpost_ref,
                 xo_ref, fso_ref,
                 ubuf_ref, vbuf_ref, acc_ref, hf_ref, *, nseq, tt):
    rows = nseq * tt

    @pl.when(pl.program_id(1) == 0)
    def _():
        fso_ref[...] = fs_ref[...]

    half = D_MODEL // 2
    ol = olat_ref[...].reshape(rows, N_HEADS * KV_RANK)
    mix = _dot(yconv_ref[...].reshape(rows, D_CONV), wo_ref[0:D_CONV, :])
    for g in range(2):
        ym = _dot(ol[:, half * g:half * (g + 1)], wuvg_ref[g]).astype(BF16)
        mix = mix + _dot(ym, wo_ref[D_CONV + 256 * g:D_CONV + 256 * (g + 1), :])
    x1 = x_ref[...].reshape(rows, D_MODEL) + _rms(mix, gpost_ref[...])
    xo_ref[...] = x1.reshape(nseq, tt, D_MODEL)
    hf_ref[...] = _rms(x1, gfpre_ref[...]).astype(BF16)
    acc_ref[...] = jnp.zeros(acc_ref.shape, F32)

    def up_proj(c, buf_ref):
        buf_ref[:, HALO - 2:HALO, :] = fso_ref[c]
        up = _dot(hf_ref[...], wup_ref[c]).reshape(nseq, tt, 2 * FF_CHUNK)
        buf_ref[:, HALO:HALO + tt, :] = up
        fso_ref[c] = up[:, tt - 2:tt, :]

    def gate_down(c, buf_ref):
        w = wfc_ref[c]
        u = (buf_ref[:, HALO - 2:HALO - 2 + tt, :] * w[0] + buf_ref[:, HALO - 1:HALO - 1 + tt, :] * w[1]
             + buf_ref[:, HALO:HALO + tt, :] * w[2] + bfc_ref[c]).reshape(rows, 2 * FF_CHUNK)
        a = u[:, 0:FF_CHUNK]
        b = u[:, FF_CHUNK:2 * FF_CHUNK]
        gated = (a * (1.0 / (1.0 + jnp.exp(-a))) * b).astype(BF16)
        acc_ref[...] += _dot(gated, wdown_ref[c])

    up_proj(0, ubuf_ref)

    def pair_body(i, carry):
        up_proj(2 * i + 1, vbuf_ref)
        gate_down(2 * i, ubuf_ref)
        up_proj(2 * i + 2, ubuf_ref)
        gate_down(2 * i + 1, vbuf_ref)
        return carry

    lax.fori_loop(0, (N_FF_CHUNKS - 1) // 2, pair_body, 0)
    gate_down(N_FF_CHUNKS - 1, ubuf_ref)
    xo_ref[...] = xo_ref[...] + _rms(acc_ref[...], gfpost_ref[...]).reshape(nseq, tt, D_MODEL)


def _post(x, yconv, olat, ffn_state, lw, *, nseq, tt):
    B, T, _ = x.shape
    nb, nt = B // nseq, T // tt
    rows = nseq * tt
    state_block = (N_FF_CHUNKS, nseq, CONV_W - 1, 2 * FF_CHUNK)
    in_specs = [
        pl.BlockSpec((nseq, tt, D_MODEL), lambda b, t: (b, t, 0)),
        pl.BlockSpec((nseq, tt, D_CONV), lambda b, t: (b, t, 0)),
        pl.BlockSpec((nseq, tt, N_HEADS * KV_RANK), lambda b, t: (b, t, 0)),
        pl.BlockSpec(state_block, lambda b, t: (0, b, 0, 0), pipeline_mode=pl.Buffered(1)),
        _const_spec((2, 4 * KV_RANK, 4 * V_HEAD)),
        _const_spec((D_MODEL, D_MODEL)),
        _const_spec((1, D_MODEL)),
        _const_spec((1, D_MODEL)),
        _const_spec((N_FF_CHUNKS, D_MODEL, 2 * FF_CHUNK)),
        _const_spec((N_FF_CHUNKS, CONV_W, 2 * FF_CHUNK)),
        _const_spec((N_FF_CHUNKS, 1, 2 * FF_CHUNK)),
        _const_spec((N_FF_CHUNKS, FF_CHUNK, D_MODEL)),
        _const_spec((1, D_MODEL)),
    ]
    return pl.pallas_call(
        functools.partial(_post_kernel, nseq=nseq, tt=tt),
        grid=(nb, nt),
        in_specs=in_specs,
        out_specs=(
            pl.BlockSpec((nseq, tt, D_MODEL), lambda b, t: (b, t, 0)),
            pl.BlockSpec(state_block, lambda b, t: (0, b, 0, 0)),
        ),
        out_shape=(
            jax.ShapeDtypeStruct((B, T, D_MODEL), F32),
            jax.ShapeDtypeStruct((N_FF_CHUNKS, B, CONV_W - 1, 2 * FF_CHUNK), F32),
        ),
        scratch_shapes=[
            pltpu.VMEM((nseq, tt + HALO, 2 * FF_CHUNK), F32),
            pltpu.VMEM((nseq, tt + HALO, 2 * FF_CHUNK), F32),
            pltpu.VMEM((rows, D_MODEL), F32),
            pltpu.VMEM((rows, D_MODEL), BF16),
        ],
        compiler_params=pltpu.CompilerParams(
            dimension_semantics=("arbitrary", "arbitrary"), vmem_limit_bytes=VMEM_LIMIT_BYTES),
        name="post_ffn",
    )(x, yconv, olat, ffn_state, lw["w_uvg"], lw["w_o"], lw["g_mix_post"], lw["g_ffn_pre"], lw["w_up"],
      lw["w_ffn_conv"], lw["b_ffn_conv"], lw["w_down"], lw["g_ffn_post"])


def _chunk_ff(a):
    lead = a.shape[:-1]
    a = a.reshape(lead + (2, N_FF_CHUNKS, FF_CHUNK))
    a = jnp.moveaxis(a, -2, 0)
    return a.reshape((N_FF_CHUNKS,) + lead + (2 * FF_CHUNK,))


def _unchunk_ff(a):
    lead = a.shape[1:-1]
    a = a.reshape((N_FF_CHUNKS,) + lead + (2, FF_CHUNK))
    a = jnp.moveaxis(a, 0, -2)
    return a.reshape(lead + (2 * D_FF,))


def _block_diag(blocks):
    n = len(blocks)
    rows = []
    for i, blk in enumerate(blocks):
        rows.append(jnp.concatenate(
            [blk if j == i else jnp.zeros((blk.shape[0], blocks[j].shape[1]), blk.dtype) for j in range(n)],
            axis=1))
    return jnp.concatenate(rows, axis=0)


def _layer_weights(l, w_in, w_conv, g_qa, w_uq, g_kva, w_uk, w_uv, w_o, g_mix_pre, g_mix_post,
                   w_up, w_ffn_conv, b_ffn_conv, w_down, g_ffn_pre, g_ffn_post):
    d_in = w_in.shape[2]
    uq = w_uq[l].reshape(Q_RANK, N_HEADS, QK_NOPE + QK_ROPE)
    uq = jnp.concatenate([uq[:, :, :QK_NOPE].reshape(Q_RANK, N_HEADS * QK_NOPE),
                          uq[:, :, QK_NOPE:].reshape(Q_RANK, N_HEADS * QK_ROPE)], axis=1)
    uk_t = jnp.transpose(w_uk[l], (1, 2, 0))
    uv = jnp.transpose(w_uv[l], (1, 0, 2))
    return {
        "w1": jnp.pad(w_in[l], ((0, 0), (0, D_IN_PAD - d_in))).astype(BF16),
        "w_conv": w_conv[l],
        "g_mix_pre": g_mix_pre[l][None],
        "g_qa": g_qa[l][None],
        "w_uq": uq.astype(BF16),
        "g_kva": g_kva[l][None],
        "w_ukp": jnp.stack([_block_diag([uk_t[2 * p], uk_t[2 * p + 1]])
                            for p in range(N_HEADS // 2)]).astype(BF16),
        "w_uvg": jnp.stack([_block_diag([uv[4 * g + j] for j in range(4)]) for g in range(2)]).astype(BF16),
        "w_o": w_o[l].astype(BF16),
        "g_mix_post": g_mix_post[l][None],
        "g_ffn_pre": g_ffn_pre[l][None],
        "w_up": _chunk_ff(w_up[l]).astype(BF16),
        "w_ffn_conv": _chunk_ff(w_ffn_conv[l]),
        "b_ffn_conv": _chunk_ff(b_ffn_conv[l][None]),
        "w_down": w_down[l].reshape(N_FF_CHUNKS, FF_CHUNK, D_MODEL).astype(BF16),
        "g_ffn_post": g_ffn_post[l][None],
    }


def _rope_tables(pos, reps):
    inv = 1.0 / (ROPE_BASE ** (jnp.arange(0, QK_ROPE, 2, dtype=F32) / QK_ROPE))
    ang = pos.astype(F32)[:, None] * inv[None, :]
    cos, sin = jnp.cos(ang), jnp.sin(ang)
    zero = jnp.zeros_like(sin)
    group = 128 // QK_ROPE
    rc = jnp.tile(jnp.concatenate([cos, cos], axis=1), (reps, group))
    rs1 = jnp.tile(jnp.concatenate([-sin, zero], axis=1), (reps, group))
    rs2 = jnp.tile(jnp.concatenate([zero, sin], axis=1), (reps, group))
    return rc, rs1, rs2


def _pick_tile(T, target):
    tt = min(T, target)
    assert T % tt == 0 and tt % 16 == 0
    return tt


def _trunk(x_prompt, x_sample, cache_ckv, cache_kpe, state_conv, state_ffn, weights):
    depth = cache_ckv.shape[0]
    Bp, Tp, _ = x_prompt.shape
    Bs, Ts, _ = x_sample.shape
    past = cache_ckv.shape[2]
    assert Ts >= CONV_W - 1 and Ts % 16 == 0

    tt = _pick_tile(Tp, 512)
    tq = _pick_tile(Tp, 128)
    tk = _pick_tile(Tp, 512)
    ns = math.gcd(Bs, SAMPLE_SEQS_PER_STEP)
    p_ropes = _rope_tables(jnp.arange(Tp), 1)
    s_ropes = _rope_tables(past + jnp.arange(Ts), ns)
    p_conv0 = jnp.zeros((Bp, CONV_W - 1, D_CONV), F32)
    p_ffn0 = jnp.zeros((N_FF_CHUNKS, Bp, CONV_W - 1, 2 * FF_CHUNK), F32)

    xp, xs = x_prompt, x_sample
    outs = [[] for _ in range(8)]
    for l in range(depth):
        lw = _layer_weights(l, *weights)
        yc_p, q_p, kcat_p, ckv_p, kpe_p, conv_p, vt_p = _mix_in(
            xp, p_conv0, lw, p_ropes, nseq=1, tt=tt, vt_tile=tk, qt_tile=tq)
        yc_s, q_s, kcat_s, ckv_s, kpe_s, conv_s = _mix_in(xs, state_conv[l], lw, s_ropes, nseq=ns, tt=Ts)
        ol_p = _attend_prompt(q_p, kcat_p, vt_p)
        ol_s = _attend_sample(q_s, cache_ckv, cache_kpe, kcat_s, l)
        xp, ffn_p = _post(xp, yc_p, ol_p, p_ffn0, lw, nseq=1, tt=tt)
        xs, ffn_s = _post(xs, yc_s, ol_s, _chunk_ff(state_ffn[l]), lw, nseq=ns, tt=Ts)
        for acc, v in zip(outs, (ckv_p, kpe_p, conv_p, _unchunk_ff(ffn_p),
                                 ckv_s, kpe_s, conv_s, _unchunk_ff(ffn_s))):
            acc.append(v)
    return (xp, xs) + tuple(jnp.stack(v) for v in outs)


def kernel(x_prompt, x_sample, cache_ckv, cache_kpe, state_conv, state_ffn, w_in, w_conv, g_qa, w_uq, g_kva, w_uk, w_uv, w_o, g_mix_pre, g_mix_post, w_up, w_ffn_conv, b_ffn_conv, w_down, g_ffn_pre, g_ffn_post):
    weights = (w_in, w_conv, g_qa, w_uq, g_kva, w_uk, w_uv, w_o, g_mix_pre, g_mix_post,
               w_up, w_ffn_conv, b_ffn_conv, w_down, g_ffn_pre, g_ffn_post)
    return _trunk(x_prompt, x_sample, cache_ckv, cache_kpe, state_conv, state_ffn, weights)
```

```python
import functools
import math

import jax
import jax.numpy as jnp
from jax import lax
from jax.experimental import pallas as pl
from jax.experimental.pallas import tpu as pltpu

D_MODEL = 1024
CHUNK = 64
CONV_W = 3
D_CONV = D_MODEL // 2
N_HEADS = 8
QK_NOPE = 64
QK_ROPE = 32
V_HEAD = 64
KV_RANK = 128
Q_RANK = 256
D_FF = 2816
ROPE_BASE = 10000.0
EPS = 1e-6
ATTN_SCALE = 1.0 / math.sqrt(QK_NOPE + QK_ROPE)
Q_SCALE = ATTN_SCALE * math.log2(math.e)

D_QK = KV_RANK + QK_ROPE
D_IN_PAD = 2048
COL_QA = 3 * D_CONV
COL_KV = COL_QA + Q_RANK
FF_CHUNK = 256
N_FF_CHUNKS = D_FF // FF_CHUNK
assert N_FF_CHUNKS % 2 == 1
ATTN_COL_GROUPS = 1
VT_ROWS = KV_RANK + 16
HALO = 8
SAMPLE_SEQS_PER_STEP = 16

VMEM_LIMIT_BYTES = 56 * 1024 * 1024

BF16 = jnp.bfloat16
F32 = jnp.float32


def _rms(x, g):
    return x * lax.rsqrt(jnp.mean(x * x, axis=-1, keepdims=True) + EPS) * g


def _dot(a, b):
    return jnp.dot(a, b, preferred_element_type=F32)


def _dot_nt(a, b):
    return lax.dot_general(a, b, (((1,), (1,)), ((), ())), preferred_element_type=F32)


def _chunk_of(pos):
    return lax.shift_right_logical(pos, CHUNK.bit_length() - 1)


def _rope(x, rc, rs1, rs2):
    return x * rc + pltpu.roll(x, 128 - QK_ROPE // 2, 1) * rs1 + pltpu.roll(x, QK_ROPE // 2, 1) * rs2


def _causal_conv(buf_ref, cur, prev, w):
    tt = cur.shape[1]
    buf_ref[:, HALO - 2:HALO, :] = prev
    buf_ref[:, HALO:HALO + tt, :] = cur
    s1 = buf_ref[:, HALO - 1:HALO - 1 + tt, :]
    s2 = buf_ref[:, HALO - 2:HALO - 2 + tt, :]
    return s2 * w[0] + s1 * w[1] + cur * w[2]


def _mix_in_kernel(x_ref, cs_ref, w1_ref, wconv_ref, gpre_ref, gqa_ref, wuq_ref, gkva_ref, wukp_ref,
                   rc_ref, rs1_ref, rs2_ref,
                   yconv_ref, q_ref, kcat_ref, ckv_ref, kpe_ref, cso_ref, *rest, nseq, tt, vt_tile, qt_tile):
    cbuf_ref = rest[-1]
    rows = nseq * tt

    @pl.when(pl.program_id(1) == 0)
    def _():
        cso_ref[...] = cs_ref[...]

    x = x_ref[...].reshape(rows, D_MODEL)
    h = _rms(x, gpre_ref[...]).astype(BF16)

    qa = _dot(h, w1_ref[:, COL_QA:COL_KV])
    kv = _dot(h, w1_ref[:, COL_KV:D_IN_PAD])
    zc = _dot(h, w1_ref[:, 0:COL_QA])
    qn = _rms(qa, gqa_ref[...]).astype(BF16)
    q = _dot(qn, wuq_ref[...])
    qls = [_dot(q[:, 128 * p:128 * (p + 1)].astype(BF16), wukp_ref[p]) * Q_SCALE for p in range(N_HEADS // 2)]

    xv = zc[:, 0:D_CONV]
    gb = zc[:, D_CONV:2 * D_CONV]
    gc = zc[:, 2 * D_CONV:3 * D_CONV]
    c = (gc * xv).reshape(nseq, tt, D_CONV)
    conv = _causal_conv(cbuf_ref, c, cso_ref[...], wconv_ref[...])
    yconv_ref[...] = (gb.reshape(nseq, tt, D_CONV) * conv).astype(BF16)
    cso_ref[...] = c[:, tt - 2:tt, :]

    rc = rc_ref[...]
    rs1 = rs1_ref[...]
    rs2 = rs2_ref[...]

    n_nope = N_HEADS * QK_NOPE
    heads_per_tile = 128 // QK_ROPE
    for p in range(N_HEADS // 2):
        ql = qls[p]
        for j in range(2):
            hd = 2 * p + j
            qh = ql[:, KV_RANK * j:KV_RANK * (j + 1)]
            if qt_tile:
                qh_t = qh.T.astype(BF16)
                for i in range(tt // qt_tile):
                    q_ref[0, i, 0:KV_RANK, qt_tile * hd:qt_tile * (hd + 1)] = (
                        qh_t[:, qt_tile * i:qt_tile * (i + 1)])
            else:
                q_ref[:, hd, :, 0:KV_RANK] = qh.reshape(nseq, tt, KV_RANK).astype(BF16)
    for half in range(N_HEADS // heads_per_tile):
        qp = _rope(q[:, n_nope + 128 * half:n_nope + 128 * (half + 1)], rc, rs1, rs2) * Q_SCALE
        qp_t = qp.T.astype(BF16) if qt_tile else None
        for j in range(heads_per_tile):
            hd = heads_per_tile * half + j
            if qt_tile:
                for i in range(tt // qt_tile):
                    q_ref[0, i, KV_RANK:D_QK, qt_tile * hd:qt_tile * (hd + 1)] = (
                        qp_t[QK_ROPE * j:QK_ROPE * (j + 1), qt_tile * i:qt_tile * (i + 1)])
            else:
                q_ref[:, hd, :, KV_RANK:D_QK] = (
                    qp[:, QK_ROPE * j:QK_ROPE * (j + 1)].reshape(nseq, tt, QK_ROPE).astype(BF16))

    ckv = _rms(kv[:, 0:KV_RANK], gkva_ref[...])
    kpe = _rope(kv[:, KV_RANK:2 * KV_RANK], rc, rs1, rs2)[:, 0:QK_ROPE]
    ckv_ref[...] = ckv.reshape(nseq, tt, KV_RANK)
    kpe_ref[...] = kpe.reshape(nseq, tt, QK_ROPE)
    kcat_ref[:, :, 0:KV_RANK] = ckv.reshape(nseq, tt, KV_RANK).astype(BF16)
    kcat_ref[:, :, KV_RANK:D_QK] = kpe.reshape(nseq, tt, QK_ROPE).astype(BF16)
    if vt_tile:
        vt_ref = rest[0]
        for i in range(tt // vt_tile):
            vt_ref[0, i, 0:KV_RANK, :] = ckv[vt_tile * i:vt_tile * (i + 1), :].T.astype(BF16)
            vt_ref[0, i, KV_RANK:VT_ROWS, :] = jnp.ones((VT_ROWS - KV_RANK, vt_tile), BF16)


def _const_spec(shape):
    nd = len(shape)
    return pl.BlockSpec(shape, lambda *_: (0,) * nd, pipeline_mode=pl.Buffered(1))


def _mix_in(x, conv_state, lw, ropes, *, nseq, tt, vt_tile=0, qt_tile=0):
    B, T, _ = x.shape
    assert not vt_tile or (nseq == 1 and tt % vt_tile == 0)
    assert not qt_tile or (nseq == 1 and tt % qt_tile == 0)
    if qt_tile:
        q_shape = jax.ShapeDtypeStruct((B, T // qt_tile, D_QK, N_HEADS * qt_tile), BF16)
        q_spec = pl.BlockSpec((1, tt // qt_tile, D_QK, N_HEADS * qt_tile), lambda b, t: (b, t, 0, 0))
    else:
        q_shape = jax.ShapeDtypeStruct((B, N_HEADS, T, D_QK), BF16)
        q_spec = pl.BlockSpec((nseq, N_HEADS, tt, D_QK), lambda b, t: (b, 0, t, 0))
    nb, nt = B // nseq, T // tt
    rows = nseq * tt
    rope_spec = pl.BlockSpec((rows, 128), lambda b, t: (t, 0))
    in_specs = [
        pl.BlockSpec((nseq, tt, D_MODEL), lambda b, t: (b, t, 0)),
        pl.BlockSpec((nseq, CONV_W - 1, D_CONV), lambda b, t: (b, 0, 0)),
        _const_spec((D_MODEL, D_IN_PAD)),
        _const_spec((CONV_W, D_CONV)),
        _const_spec((1, D_MODEL)),
        _const_spec((1, Q_RANK)),
        _const_spec((Q_RANK, N_HEADS * (QK_NOPE + QK_ROPE))),
        _const_spec((1, KV_RANK)),
        _const_spec((N_HEADS // 2, 128, 256)),
        rope_spec, rope_spec, rope_spec,
    ]
    out_shape = [
        jax.ShapeDtypeStruct((B, T, D_CONV), BF16),
        q_shape,
        jax.ShapeDtypeStruct((B, T, D_QK), BF16),
        jax.ShapeDtypeStruct((B, T, KV_RANK), F32),
        jax.ShapeDtypeStruct((B, T, QK_ROPE), F32),
        jax.ShapeDtypeStruct((B, CONV_W - 1, D_CONV), F32),
    ]
    out_specs = [
        pl.BlockSpec((nseq, tt, D_CONV), lambda b, t: (b, t, 0)),
        q_spec,
        pl.BlockSpec((nseq, tt, D_QK), lambda b, t: (b, t, 0)),
        pl.BlockSpec((nseq, tt, KV_RANK), lambda b, t: (b, t, 0)),
        pl.BlockSpec((nseq, tt, QK_ROPE), lambda b, t: (b, t, 0)),
        pl.BlockSpec((nseq, CONV_W - 1, D_CONV), lambda b, t: (b, 0, 0)),
    ]
    if vt_tile:
        out_shape.append(jax.ShapeDtypeStruct((B, T // vt_tile, VT_ROWS, vt_tile), BF16))
        out_specs.append(pl.BlockSpec((1, tt // vt_tile, VT_ROWS, vt_tile), lambda b, t: (b, t, 0, 0)))
    return pl.pallas_call(
        functools.partial(_mix_in_kernel, nseq=nseq, tt=tt, vt_tile=vt_tile, qt_tile=qt_tile),
        grid=(nb, nt),
        in_specs=in_specs,
        out_specs=out_specs,
        out_shape=out_shape,
        scratch_shapes=[pltpu.VMEM((nseq, tt + HALO, D_CONV), F32)],
        compiler_params=pltpu.CompilerParams(
            dimension_semantics=("arbitrary", "arbitrary"), vmem_limit_bytes=VMEM_LIMIT_BYTES),
        name="mix_in",
    )(x, conv_state, lw["w1"], lw["w_conv"], lw["g_mix_pre"], lw["g_qa"], lw["w_uq"], lw["g_kva"],
      lw["w_ukp"], *ropes)


def _attend_prompt_kernel(q_ref, k_ref, vt_ref, o_ref, m_ref, acc_ref, sa_ref, sb_ref, *, tq, tk):
    cols = N_HEADS * tq
    gw = cols // ATTN_COL_GROUPS
    q0 = pl.program_id(1) * tq
    n_last = q0 // tk

    m_ref[...] = jnp.full(m_ref.shape, -jnp.inf, F32)
    acc_ref[...] = jnp.zeros(acc_ref.shape, F32)

    def scores(j, s_ref):
        k0 = pl.multiple_of(j * tk, tk)
        s_ref[...] = _dot(k_ref[0, pl.ds(k0, tk), :], q_ref[0, 0])

    def consume(j, s_ref, masked):
        vt = vt_ref[0, j]
        for g in range(ATTN_COL_GROUPS):
            sl = slice(gw * g, gw * (g + 1))
            s = s_ref[:, sl]
            if masked:
                k_pos = j * tk + lax.broadcasted_iota(jnp.int32, (tk, gw), 0)
                q_pos = q0 + (lax.broadcasted_iota(jnp.int32, (tk, gw), 1) & (tq - 1))
                s = jnp.where(_chunk_of(k_pos) <= _chunk_of(q_pos), s, -jnp.inf)
            m_prev = m_ref[:, sl]
            m_new = jnp.maximum(m_prev, jnp.max(s, axis=0, keepdims=True))
            p = jnp.exp2(s - m_new).astype(BF16)
            acc_ref[:, sl] = jnp.exp2(m_prev - m_new) * acc_ref[:, sl] + _dot(vt, p)
            m_ref[:, sl] = m_new

    scores(0, sa_ref)

    def pair_body(i, carry):
        scores(2 * i + 1, sb_ref)
        consume(2 * i, sa_ref, False)
        scores(2 * i + 2, sa_ref)
        consume(2 * i + 1, sb_ref, False)
        return carry

    lax.fori_loop(0, n_last // 2, pair_body, 0)

    @pl.when(n_last % 2 == 1)
    def _():
        scores(n_last, sb_ref)
        consume(n_last - 1, sa_ref, False)
        consume(n_last, sb_ref, True)

    @pl.when(n_last % 2 == 0)
    def _():
        consume(n_last, sa_ref, True)

    o = acc_ref[0:KV_RANK, :] / acc_ref[KV_RANK:KV_RANK + 1, :]
    for h in range(N_HEADS):
        o_ref[0, :, KV_RANK * h:KV_RANK * (h + 1)] = o[:, tq * h:tq * (h + 1)].T.astype(BF16)


def _attend_prompt(q_t, kcat, vt):
    B, T, _ = kcat.shape
    tk = vt.shape[3]
    cols = q_t.shape[3]
    tq = cols // N_HEADS
    assert tq % CHUNK == 0 and tq & (tq - 1) == 0 and T % tk == 0 and tk % tq == 0
    return pl.pallas_call(
        functools.partial(_attend_prompt_kernel, tq=tq, tk=tk),
        grid=(B, T // tq),
        in_specs=[
            pl.BlockSpec((1, 1, D_QK, cols), lambda b, i: (b, i, 0, 0)),
            pl.BlockSpec((1, T, D_QK), lambda b, i: (b, 0, 0)),
            pl.BlockSpec((1, T // tk, VT_ROWS, tk), lambda b, i: (b, 0, 0, 0)),
        ],
        out_specs=pl.BlockSpec((1, tq, N_HEADS * KV_RANK), lambda b, i: (b, i, 0)),
        out_shape=jax.ShapeDtypeStruct((B, T, N_HEADS * KV_RANK), BF16),
        scratch_shapes=[
            pltpu.VMEM((1, cols), F32),
            pltpu.VMEM((VT_ROWS, cols), F32),
            pltpu.VMEM((tk, cols), F32),
            pltpu.VMEM((tk, cols), F32),
        ],
        compiler_params=pltpu.CompilerParams(
            dimension_semantics=("arbitrary", "arbitrary"), vmem_limit_bytes=VMEM_LIMIT_BYTES),
        name="attend_prompt",
    )(q_t, kcat, vt)


def _attend_sample_kernel(q_ref, ckv_ref, kpe_ref, kn_ref, o_ref, *, ts, past):
    rows = N_HEADS * ts
    q = q_ref[0].reshape(rows, D_QK)
    ck = ckv_ref[0, 0].astype(BF16)
    kp = kpe_ref[0, 0].astype(BF16)
    kn = kn_ref[0]
    s_old = _dot_nt(q[:, 0:KV_RANK], ck) + _dot_nt(q[:, KV_RANK:D_QK], kp)
    s_new = _dot_nt(q, kn)
    q_pos = past + (lax.broadcasted_iota(jnp.int32, (rows, ts), 0) & (ts - 1))
    k_pos = past + lax.broadcasted_iota(jnp.int32, (rows, ts), 1)
    s_new = jnp.where(_chunk_of(k_pos) <= _chunk_of(q_pos), s_new, -jnp.inf)
    m = jnp.maximum(jnp.max(s_old, axis=1, keepdims=True), jnp.max(s_new, axis=1, keepdims=True))
    p_old = jnp.exp2(s_old - m)
    p_new = jnp.exp2(s_new - m)
    l = jnp.sum(p_old, axis=1, keepdims=True) + jnp.sum(p_new, axis=1, keepdims=True)
    o = (_dot(p_old.astype(BF16), ck) + _dot(p_new.astype(BF16), kn[:, 0:KV_RANK])) / l
    for h in range(N_HEADS):
        o_ref[0, :, KV_RANK * h:KV_RANK * (h + 1)] = o[ts * h:ts * (h + 1), :].astype(BF16)


def _attend_sample(q, cache_ckv, cache_kpe, kcat, layer):
    B, _, ts, _ = q.shape
    past = cache_ckv.shape[2]
    return pl.pallas_call(
        functools.partial(_attend_sample_kernel, ts=ts, past=past),
        grid=(B,),
        in_specs=[
            pl.BlockSpec((1, N_HEADS, ts, D_QK), lambda b: (b, 0, 0, 0)),
            pl.BlockSpec((1, 1, past, KV_RANK), lambda b: (layer, b, 0, 0)),
            pl.BlockSpec((1, 1, past, QK_ROPE), lambda b: (layer, b, 0, 0)),
            pl.BlockSpec((1, ts, D_QK), lambda b: (b, 0, 0)),
        ],
        out_specs=pl.BlockSpec((1, ts, N_HEADS * KV_RANK), lambda b: (b, 0, 0)),
        out_shape=jax.ShapeDtypeStruct((B, ts, N_HEADS * KV_RANK), BF16),
        compiler_params=pltpu.CompilerParams(
            dimension_semantics=("arbitrary",), vmem_limit_bytes=VMEM_LIMIT_BYTES),
        name="attend_sample",
    )(q, cache_ckv, cache_kpe, kcat)


def _post_kernel(x_ref, yconv_ref, olat_ref, fs_ref, wuvg_ref, wo_ref, gpost_ref, gfpre_ref, wup_ref,
                 wfc_ref, bfc_ref, wdown_ref, gfpost_ref,
                 xo_ref, fso_ref,
                 ubuf_ref, vbuf_ref, acc_ref, hf_ref, *, nseq, tt):
    rows = nseq * tt

    @pl.when(pl.program_id(1) == 0)
    def _():
        fso_ref[...] = fs_ref[...]

    half = D_MODEL // 2
    ol = olat_ref[...].reshape(rows, N_HEADS * KV_RANK)
    yms = [_dot(ol[:, half * g:half * (g + 1)], wuvg_ref[g]).astype(BF16) for g in range(2)]
    mix = _dot(yconv_ref[...].reshape(rows, D_CONV), wo_ref[0:D_CONV, :])
    for g in range(2):
        mix = mix + _dot(yms[g], wo_ref[D_CONV + 256 * g:D_CONV + 256 * (g + 1), :])
    x1 = x_ref[...].reshape(rows, D_MODEL) + _rms(mix, gpost_ref[...])
    xo_ref[...] = x1.reshape(nseq, tt, D_MODEL)
    hf_ref[...] = _rms(x1, gfpre_ref[...]).astype(BF16)
    acc_ref[...] = jnp.zeros(acc_ref.shape, F32)

    def up_proj(c, buf_ref):
        buf_ref[:, HALO - 2:HALO, :] = fso_ref[c]
        for half in range(2):
            cols = slice(FF_CHUNK * half, FF_CHUNK * (half + 1))
            up = _dot(hf_ref[...], wup_ref[c, :, cols]).reshape(nseq, tt, FF_CHUNK)
            buf_ref[:, HALO:HALO + tt, cols] = up
            fso_ref[c, :, :, cols] = up[:, tt - 2:tt, :]

    def gate_down(c, buf_ref):
        w = wfc_ref[c]
        u = (buf_ref[:, HALO - 2:HALO - 2 + tt, :] * w[0] + buf_ref[:, HALO - 1:HALO - 1 + tt, :] * w[1]
             + buf_ref[:, HALO:HALO + tt, :] * w[2] + bfc_ref[c]).reshape(rows, 2 * FF_CHUNK)
        a = u[:, 0:FF_CHUNK]
        b = u[:, FF_CHUNK:2 * FF_CHUNK]
        gated = (a * (1.0 / (1.0 + jnp.exp(-a))) * b).astype(BF16)
        acc_ref[...] += _dot(gated, wdown_ref[c])

    up_proj(0, ubuf_ref)

    def pair_body(i, carry):
        up_proj(2 * i + 1, vbuf_ref)
        gate_down(2 * i, ubuf_ref)
        up_proj(2 * i + 2, ubuf_ref)
        gate_down(2 * i + 1, vbuf_ref)
        return carry

    lax.fori_loop(0, (N_FF_CHUNKS - 1) // 2, pair_body, 0)
    gate_down(N_FF_CHUNKS - 1, ubuf_ref)
    xo_ref[...] = xo_ref[...] + _rms(acc_ref[...], gfpost_ref[...]).reshape(nseq, tt, D_MODEL)


def _post(x, yconv, olat, ffn_state, lw, *, nseq, tt):
    B, T, _ = x.shape
    nb, nt = B // nseq, T // tt
    rows = nseq * tt
    state_block = (N_FF_CHUNKS, nseq, CONV_W - 1, 2 * FF_CHUNK)
    in_specs = [
        pl.BlockSpec((nseq, tt, D_MODEL), lambda b, t: (b, t, 0)),
        pl.BlockSpec((nseq, tt, D_CONV), lambda b, t: (b, t, 0)),
        pl.BlockSpec((nseq, tt, N_HEADS * KV_RANK), lambda b, t: (b, t, 0)),
        pl.BlockSpec(state_block, lambda b, t: (0, b, 0, 0), pipeline_mode=pl.Buffered(1)),
        _const_spec((2, 4 * KV_RANK, 4 * V_HEAD)),
        _const_spec((D_MODEL, D_MODEL)),
        _const_spec((1, D_MODEL)),
        _const_spec((1, D_MODEL)),
        _const_spec((N_FF_CHUNKS, D_MODEL, 2 * FF_CHUNK)),
        _const_spec((N_FF_CHUNKS, CONV_W, 2 * FF_CHUNK)),
        _const_spec((N_FF_CHUNKS, 1, 2 * FF_CHUNK)),
        _const_spec((N_FF_CHUNKS, FF_CHUNK, D_MODEL)),
        _const_spec((1, D_MODEL)),
    ]
    return pl.pallas_call(
        functools.partial(_post_kernel, nseq=nseq, tt=tt),
        grid=(nb, nt),
        in_specs=in_specs,
        out_specs=(
            pl.BlockSpec((nseq, tt, D_MODEL), lambda b, t: (b, t, 0)),
            pl.BlockSpec(state_block, lambda b, t: (0, b, 0, 0)),
        ),
        out_shape=(
            jax.ShapeDtypeStruct((B, T, D_MODEL), F32),
            jax.ShapeDtypeStruct((N_FF_CHUNKS, B, CONV_W - 1, 2 * FF_CHUNK), F32),
        ),
        scratch_shapes=[
            pltpu.VMEM((nseq, tt + HALO, 2 * FF_CHUNK), F32),
            pltpu.VMEM((nseq, tt + HALO, 2 * FF_CHUNK), F32),
            pltpu.VMEM((rows, D_MODEL), F32),
            pltpu.VMEM((rows, D_MODEL), BF16),
        ],
        compiler_params=pltpu.CompilerParams(
            dimension_semantics=("arbitrary", "arbitrary"), vmem_limit_bytes=VMEM_LIMIT_BYTES),
        name="post_ffn",
    )(x, yconv, olat, ffn_state, lw["w_uvg"], lw["w_o"], lw["g_mix_post"], lw["g_ffn_pre"], lw["w_up"],
      lw["w_ffn_conv"], lw["b_ffn_conv"], lw["w_down"], lw["g_ffn_post"])


def _chunk_ff(a):
    lead = a.shape[:-1]
    a = a.reshape(lead + (2, N_FF_CHUNKS, FF_CHUNK))
    a = jnp.moveaxis(a, -2, 0)
    return a.reshape((N_FF_CHUNKS,) + lead + (2 * FF_CHUNK,))


def _unchunk_ff(a):
    lead = a.shape[1:-1]
    a = a.reshape((N_FF_CHUNKS,) + lead + (2, FF_CHUNK))
    a = jnp.moveaxis(a, 0, -2)
    return a.reshape(lead + (2 * D_FF,))


def _block_diag(blocks):
    n = len(blocks)
    rows = []
    for i, blk in enumerate(blocks):
        rows.append(jnp.concatenate(
            [blk if j == i else jnp.zeros((blk.shape[0], blocks[j].shape[1]), blk.dtype) for j in range(n)],
            axis=1))
    return jnp.concatenate(rows, axis=0)


def _layer_weights(l, w_in, w_conv, g_qa, w_uq, g_kva, w_uk, w_uv, w_o, g_mix_pre, g_mix_post,
                   w_up, w_ffn_conv, b_ffn_conv, w_down, g_ffn_pre, g_ffn_post):
    d_in = w_in.shape[2]
    uq = w_uq[l].reshape(Q_RANK, N_HEADS, QK_NOPE + QK_ROPE)
    uq = jnp.concatenate([uq[:, :, :QK_NOPE].reshape(Q_RANK, N_HEADS * QK_NOPE),
                          uq[:, :, QK_NOPE:].reshape(Q_RANK, N_HEADS * QK_ROPE)], axis=1)
    uk_t = jnp.transpose(w_uk[l], (1, 2, 0))
    uv = jnp.transpose(w_uv[l], (1, 0, 2))
    return {
        "w1": jnp.pad(w_in[l], ((0, 0), (0, D_IN_PAD - d_in))).astype(BF16),
        "w_conv": w_conv[l],
        "g_mix_pre": g_mix_pre[l][None],
        "g_qa": g_qa[l][None],
        "w_uq": uq.astype(BF16),
        "g_kva": g_kva[l][None],
        "w_ukp": jnp.stack([_block_diag([uk_t[2 * p], uk_t[2 * p + 1]])
                            for p in range(N_HEADS // 2)]).astype(BF16),
        "w_uvg": jnp.stack([_block_diag([uv[4 * g + j] for j in range(4)]) for g in range(2)]).astype(BF16),
        "w_o": w_o[l].astype(BF16),
        "g_mix_post": g_mix_post[l][None],
        "g_ffn_pre": g_ffn_pre[l][None],
        "w_up": _chunk_ff(w_up[l]).astype(BF16),
        "w_ffn_conv": _chunk_ff(w_ffn_conv[l]),
        "b_ffn_conv": _chunk_ff(b_ffn_conv[l][None]),
        "w_down": w_down[l].reshape(N_FF_CHUNKS, FF_CHUNK, D_MODEL).astype(BF16),
        "g_ffn_post": g_ffn_post[l][None],
    }


def _rope_tables(pos, reps):
    inv = 1.0 / (ROPE_BASE ** (jnp.arange(0, QK_ROPE, 2, dtype=F32) / QK_ROPE))
    ang = pos.astype(F32)[:, None] * inv[None, :]
    cos, sin = jnp.cos(ang), jnp.sin(ang)
    zero = jnp.zeros_like(sin)
    group = 128 // QK_ROPE
    rc = jnp.tile(jnp.concatenate([cos, cos], axis=1), (reps, group))
    rs1 = jnp.tile(jnp.concatenate([-sin, zero], axis=1), (reps, group))
    rs2 = jnp.tile(jnp.concatenate([zero, sin], axis=1), (reps, group))
    return rc, rs1, rs2


def _pick_tile(T, target):
    tt = min(T, target)
    assert T % tt == 0 and tt % 16 == 0
    return tt


def _trunk(x_prompt, x_sample, cache_ckv, cache_kpe, state_conv, state_ffn, weights):
    depth = cache_ckv.shape[0]
    Bp, Tp, _ = x_prompt.shape
    Bs, Ts, _ = x_sample.shape
    past = cache_ckv.shape[2]
    assert Ts >= CONV_W - 1 and Ts % 16 == 0

    tt = _pick_tile(Tp, 512)
    tq = _pick_tile(Tp, 256)
    tk = _pick_tile(Tp, 512)
    ns = math.gcd(Bs, SAMPLE_SEQS_PER_STEP)
    p_ropes = _rope_tables(jnp.arange(Tp), 1)
    s_ropes = _rope_tables(past + jnp.arange(Ts), ns)
    p_conv0 = jnp.zeros((Bp, CONV_W - 1, D_CONV), F32)
    p_ffn0 = jnp.zeros((N_FF_CHUNKS, Bp, CONV_W - 1, 2 * FF_CHUNK), F32)

    xp, xs = x_prompt, x_sample
    outs = [[] for _ in range(8)]
    for l in range(depth):
        lw = _layer_weights(l, *weights)
        yc_p, q_p, kcat_p, ckv_p, kpe_p, conv_p, vt_p = _mix_in(
            xp, p_conv0, lw, p_ropes, nseq=1, tt=tt, vt_tile=tk, qt_tile=tq)
        yc_s, q_s, kcat_s, ckv_s, kpe_s, conv_s = _mix_in(xs, state_conv[l], lw, s_ropes, nseq=ns, tt=Ts)
        ol_p = _attend_prompt(q_p, kcat_p, vt_p)
        ol_s = _attend_sample(q_s, cache_ckv, cache_kpe, kcat_s, l)
        xp, ffn_p = _post(xp, yc_p, ol_p, p_ffn0, lw, nseq=1, tt=tt)
        xs, ffn_s = _post(xs, yc_s, ol_s, _chunk_ff(state_ffn[l]), lw, nseq=ns, tt=Ts)
        for acc, v in zip(outs, (ckv_p, kpe_p, conv_p, _unchunk_ff(ffn_p),
                                 ckv_s, kpe_s, conv_s, _unchunk_ff(ffn_s))):
            acc.append(v)
    return (xp, xs) + tuple(jnp.stack(v) for v in outs)


def kernel(x_prompt, x_sample, cache_ckv, cache_kpe, state_conv, state_ffn, w_in, w_conv, g_qa, w_uq, g_kva, w_uk, w_uv, w_o, g_mix_pre, g_mix_post, w_up, w_ffn_conv, b_ffn_conv, w_down, g_ffn_pre, g_ffn_post):
    weights = (w_in, w_conv, g_qa, w_uq, g_kva, w_uk, w_uv, w_o, g_mix_pre, g_mix_post,
               w_up, w_ffn_conv, b_ffn_conv, w_down, g_ffn_pre, g_ffn_post)
    return _trunk(x_prompt, x_sample, cache_ckv, cache_kpe, state_conv, state_ffn, weights)
```

```python
import functools
import math

import jax
import jax.numpy as jnp
from jax import lax
from jax.experimental import pallas as pl
from jax.experimental.pallas import tpu as pltpu

D_MODEL = 1024
CHUNK = 64
CONV_W = 3
D_CONV = D_MODEL // 2
N_HEADS = 8
QK_NOPE = 64
QK_ROPE = 32
V_HEAD = 64
KV_RANK = 128
Q_RANK = 256
D_FF = 2816
ROPE_BASE = 10000.0
EPS = 1e-6
ATTN_SCALE = 1.0 / math.sqrt(QK_NOPE + QK_ROPE)
Q_SCALE = ATTN_SCALE * math.log2(math.e)

D_QK = KV_RANK + QK_ROPE
D_IN_PAD = 2048
COL_QA = 3 * D_CONV
COL_KV = COL_QA + Q_RANK
FF_CHUNK = 1408
FF_SPLITS = tuple((o, min(FF_CHUNK, D_FF - o)) for o in range(0, D_FF, FF_CHUNK))
ATTN_COL_GROUPS = 1
VT_ROWS = KV_RANK + 16
HALO = 8
SAMPLE_SEQS_PER_STEP = 16

VMEM_LIMIT_BYTES = 56 * 1024 * 1024

BF16 = jnp.bfloat16
F32 = jnp.float32


def _rms(x, g):
    return x * lax.rsqrt(jnp.mean(x * x, axis=-1, keepdims=True) + EPS) * g


def _dot(a, b):
    return jnp.dot(a, b, preferred_element_type=F32)


def _dot_nt(a, b):
    return lax.dot_general(a, b, (((1,), (1,)), ((), ())), preferred_element_type=F32)


def _chunk_of(pos):
    return lax.shift_right_logical(pos, CHUNK.bit_length() - 1)


def _rope(x, rc, rs1, rs2):
    return x * rc + pltpu.roll(x, 128 - QK_ROPE // 2, 1) * rs1 + pltpu.roll(x, QK_ROPE // 2, 1) * rs2


def _causal_conv(buf_ref, cur, prev, w):
    tt = cur.shape[1]
    buf_ref[:, HALO - 2:HALO, :] = prev
    buf_ref[:, HALO:HALO + tt, :] = cur
    s1 = buf_ref[:, HALO - 1:HALO - 1 + tt, :]
    s2 = buf_ref[:, HALO - 2:HALO - 2 + tt, :]
    return s2 * w[0] + s1 * w[1] + cur * w[2]


def _mix_in_kernel(x_ref, cs_ref, w1_ref, wconv_ref, gpre_ref, gqa_ref, wuq_ref, gkva_ref, wukp_ref,
                   rc_ref, rs1_ref, rs2_ref,
                   yconv_ref, q_ref, kcat_ref, ckv_ref, kpe_ref, cso_ref, *rest, nseq, tt, vt_tile, qt_tile):
    cbuf_ref = rest[-1]
    rows = nseq * tt

    @pl.when(pl.program_id(1) == 0)
    def _():
        cso_ref[...] = cs_ref[...]

    x = x_ref[...].reshape(rows, D_MODEL)
    h = _rms(x, gpre_ref[...]).astype(BF16)

    qa = _dot(h, w1_ref[:, COL_QA:COL_KV])
    kv = _dot(h, w1_ref[:, COL_KV:D_IN_PAD])
    zc = _dot(h, w1_ref[:, 0:COL_QA])
    qn = _rms(qa, gqa_ref[...]).astype(BF16)
    q = _dot(qn, wuq_ref[...])
    qls = [_dot(q[:, 128 * p:128 * (p + 1)].astype(BF16), wukp_ref[p]) * Q_SCALE for p in range(N_HEADS // 2)]

    xv = zc[:, 0:D_CONV]
    gb = zc[:, D_CONV:2 * D_CONV]
    gc = zc[:, 2 * D_CONV:3 * D_CONV]
    c = (gc * xv).reshape(nseq, tt, D_CONV)
    conv = _causal_conv(cbuf_ref, c, cso_ref[...], wconv_ref[...])
    yconv_ref[...] = (gb.reshape(nseq, tt, D_CONV) * conv).astype(BF16)
    cso_ref[...] = c[:, tt - 2:tt, :]

    rc = rc_ref[...]
    rs1 = rs1_ref[...]
    rs2 = rs2_ref[...]

    n_nope = N_HEADS * QK_NOPE
    heads_per_tile = 128 // QK_ROPE
    for p in range(N_HEADS // 2):
        ql = qls[p]
        for j in range(2):
            hd = 2 * p + j
            qh = ql[:, KV_RANK * j:KV_RANK * (j + 1)]
            if qt_tile:
                qh_t = qh.T.astype(BF16)
                for i in range(tt // qt_tile):
                    q_ref[0, i, 0:KV_RANK, qt_tile * hd:qt_tile * (hd + 1)] = (
                        qh_t[:, qt_tile * i:qt_tile * (i + 1)])
            else:
                q_ref[:, hd, :, 0:KV_RANK] = qh.reshape(nseq, tt, KV_RANK).astype(BF16)
    for half in range(N_HEADS // heads_per_tile):
        qp = _rope(q[:, n_nope + 128 * half:n_nope + 128 * (half + 1)], rc, rs1, rs2) * Q_SCALE
        qp_t = qp.T.astype(BF16) if qt_tile else None
        for j in range(heads_per_tile):
            hd = heads_per_tile * half + j
            if qt_tile:
                for i in range(tt // qt_tile):
                    q_ref[0, i, KV_RANK:D_QK, qt_tile * hd:qt_tile * (hd + 1)] = (
                        qp_t[QK_ROPE * j:QK_ROPE * (j + 1), qt_tile * i:qt_tile * (i + 1)])
            else:
                q_ref[:, hd, :, KV_RANK:D_QK] = (
                    qp[:, QK_ROPE * j:QK_ROPE * (j + 1)].reshape(nseq, tt, QK_ROPE).astype(BF16))

    ckv = _rms(kv[:, 0:KV_RANK], gkva_ref[...])
    kpe = _rope(kv[:, KV_RANK:2 * KV_RANK], rc, rs1, rs2)[:, 0:QK_ROPE]
    ckv_ref[...] = ckv.reshape(nseq, tt, KV_RANK)
    kpe_ref[...] = kpe.reshape(nseq, tt, QK_ROPE)
    kcat_ref[:, :, 0:KV_RANK] = ckv.reshape(nseq, tt, KV_RANK).astype(BF16)
    kcat_ref[:, :, KV_RANK:D_QK] = kpe.reshape(nseq, tt, QK_ROPE).astype(BF16)
    if vt_tile:
        vt_ref = rest[0]
        for i in range(tt // vt_tile):
            vt_ref[0, i, 0:KV_RANK, :] = ckv[vt_tile * i:vt_tile * (i + 1), :].T.astype(BF16)
            vt_ref[0, i, KV_RANK:VT_ROWS, :] = jnp.ones((VT_ROWS - KV_RANK, vt_tile), BF16)


def _const_spec(shape):
    nd = len(shape)
    return pl.BlockSpec(shape, lambda *_: (0,) * nd, pipeline_mode=pl.Buffered(1))


def _mix_in(x, conv_state, lw, ropes, *, nseq, tt, vt_tile=0, qt_tile=0):
    B, T, _ = x.shape
    assert not vt_tile or (nseq == 1 and tt % vt_tile == 0)
    assert not qt_tile or (nseq == 1 and tt % qt_tile == 0)
    if qt_tile:
        q_shape = jax.ShapeDtypeStruct((B, T // qt_tile, D_QK, N_HEADS * qt_tile), BF16)
        q_spec = pl.BlockSpec((1, tt // qt_tile, D_QK, N_HEADS * qt_tile), lambda b, t: (b, t, 0, 0))
    else:
        q_shape = jax.ShapeDtypeStruct((B, N_HEADS, T, D_QK), BF16)
        q_spec = pl.BlockSpec((nseq, N_HEADS, tt, D_QK), lambda b, t: (b, 0, t, 0))
    nb, nt = B // nseq, T // tt
    rows = nseq * tt
    rope_spec = pl.BlockSpec((rows, 128), lambda b, t: (t, 0))
    in_specs = [
        pl.BlockSpec((nseq, tt, D_MODEL), lambda b, t: (b, t, 0)),
        pl.BlockSpec((nseq, CONV_W - 1, D_CONV), lambda b, t: (b, 0, 0)),
        _const_spec((D_MODEL, D_IN_PAD)),
        _const_spec((CONV_W, D_CONV)),
        _const_spec((1, D_MODEL)),
        _const_spec((1, Q_RANK)),
        _const_spec((Q_RANK, N_HEADS * (QK_NOPE + QK_ROPE))),
        _const_spec((1, KV_RANK)),
        _const_spec((N_HEADS // 2, 128, 256)),
        rope_spec, rope_spec, rope_spec,
    ]
    out_shape = [
        jax.ShapeDtypeStruct((B, T, D_CONV), BF16),
        q_shape,
        jax.ShapeDtypeStruct((B, T, D_QK), BF16),
        jax.ShapeDtypeStruct((B, T, KV_RANK), F32),
        jax.ShapeDtypeStruct((B, T, QK_ROPE), F32),
        jax.ShapeDtypeStruct((B, CONV_W - 1, D_CONV), F32),
    ]
    out_specs = [
        pl.BlockSpec((nseq, tt, D_CONV), lambda b, t: (b, t, 0)),
        q_spec,
        pl.BlockSpec((nseq, tt, D_QK), lambda b, t: (b, t, 0)),
        pl.BlockSpec((nseq, tt, KV_RANK), lambda b, t: (b, t, 0)),
        pl.BlockSpec((nseq, tt, QK_ROPE), lambda b, t: (b, t, 0)),
        pl.BlockSpec((nseq, CONV_W - 1, D_CONV), lambda b, t: (b, 0, 0)),
    ]
    if vt_tile:
        out_shape.append(jax.ShapeDtypeStruct((B, T // vt_tile, VT_ROWS, vt_tile), BF16))
        out_specs.append(pl.BlockSpec((1, tt // vt_tile, VT_ROWS, vt_tile), lambda b, t: (b, t, 0, 0)))
    return pl.pallas_call(
        functools.partial(_mix_in_kernel, nseq=nseq, tt=tt, vt_tile=vt_tile, qt_tile=qt_tile),
        grid=(nb, nt),
        in_specs=in_specs,
        out_specs=out_specs,
        out_shape=out_shape,
        scratch_shapes=[pltpu.VMEM((nseq, tt + HALO, D_CONV), F32)],
        compiler_params=pltpu.CompilerParams(
            dimension_semantics=("arbitrary", "arbitrary"), vmem_limit_bytes=VMEM_LIMIT_BYTES),
        name="mix_in",
    )(x, conv_state, lw["w1"], lw["w_conv"], lw["g_mix_pre"], lw["g_qa"], lw["w_uq"], lw["g_kva"],
      lw["w_ukp"], *ropes)


def _attend_prompt_kernel(q_ref, k_ref, vt_ref, o_ref, m_ref, acc_ref, sa_ref, sb_ref, *, tq, tk):
    cols = N_HEADS * tq
    gw = cols // ATTN_COL_GROUPS
    q0 = pl.program_id(1) * tq
    n_last = q0 // tk

    m_ref[...] = jnp.full(m_ref.shape, -jnp.inf, F32)
    acc_ref[...] = jnp.zeros(acc_ref.shape, F32)

    def scores(j, s_ref):
        k0 = pl.multiple_of(j * tk, tk)
        s_ref[...] = _dot(k_ref[0, pl.ds(k0, tk), :], q_ref[0, 0])

    def consume(j, s_ref, masked):
        vt = vt_ref[0, j]
        for g in range(ATTN_COL_GROUPS):
            sl = slice(gw * g, gw * (g + 1))
            s = s_ref[:, sl]
            if masked:
                k_pos = j * tk + lax.broadcasted_iota(jnp.int32, (tk, gw), 0)
                q_pos = q0 + (lax.broadcasted_iota(jnp.int32, (tk, gw), 1) & (tq - 1))
                s = jnp.where(_chunk_of(k_pos) <= _chunk_of(q_pos), s, -jnp.inf)
            m_prev = m_ref[:, sl]
            m_new = jnp.maximum(m_prev, jnp.max(s, axis=0, keepdims=True))
            p = jnp.exp2(s - m_new).astype(BF16)
            acc_ref[:, sl] = jnp.exp2(m_prev - m_new) * acc_ref[:, sl] + _dot(vt, p)
            m_ref[:, sl] = m_new

    scores(0, sa_ref)

    def pair_body(i, carry):
        scores(2 * i + 1, sb_ref)
        consume(2 * i, sa_ref, False)
        scores(2 * i + 2, sa_ref)
        consume(2 * i + 1, sb_ref, False)
        return carry

    lax.fori_loop(0, n_last // 2, pair_body, 0)

    @pl.when(n_last % 2 == 1)
    def _():
        scores(n_last, sb_ref)
        consume(n_last - 1, sa_ref, False)
        consume(n_last, sb_ref, True)

    @pl.when(n_last % 2 == 0)
    def _():
        consume(n_last, sa_ref, True)

    o = acc_ref[0:KV_RANK, :] / acc_ref[KV_RANK:KV_RANK + 1, :]
    for h in range(N_HEADS):
        o_ref[0, :, KV_RANK * h:KV_RANK * (h + 1)] = o[:, tq * h:tq * (h + 1)].T.astype(BF16)


def _attend_prompt(q_t, kcat, vt):
    B, T, _ = kcat.shape
    tk = vt.shape[3]
    cols = q_t.shape[3]
    tq = cols // N_HEADS
    assert tq % CHUNK == 0 and tq & (tq - 1) == 0 and T % tk == 0 and tk % tq == 0
    return pl.pallas_call(
        functools.partial(_attend_prompt_kernel, tq=tq, tk=tk),
        grid=(B, T // tq),
        in_specs=[
            pl.BlockSpec((1, 1, D_QK, cols), lambda b, i: (b, i, 0, 0)),
            pl.BlockSpec((1, T, D_QK), lambda b, i: (b, 0, 0)),
            pl.BlockSpec((1, T // tk, VT_ROWS, tk), lambda b, i: (b, 0, 0, 0)),
        ],
        out_specs=pl.BlockSpec((1, tq, N_HEADS * KV_RANK), lambda b, i: (b, i, 0)),
        out_shape=jax.ShapeDtypeStruct((B, T, N_HEADS * KV_RANK), BF16),
        scratch_shapes=[
            pltpu.VMEM((1, cols), F32),
            pltpu.VMEM((VT_ROWS, cols), F32),
            pltpu.VMEM((tk, cols), F32),
            pltpu.VMEM((tk, cols), F32),
        ],
        compiler_params=pltpu.CompilerParams(
            dimension_semantics=("arbitrary", "arbitrary"), vmem_limit_bytes=VMEM_LIMIT_BYTES),
        name="attend_prompt",
    )(q_t, kcat, vt)


def _attend_sample_kernel(q_ref, ckv_ref, kpe_ref, kn_ref, o_ref, *, ts, past):
    rows = N_HEADS * ts
    q = q_ref[0].reshape(rows, D_QK)
    ck = ckv_ref[0, 0].astype(BF16)
    kp = kpe_ref[0, 0].astype(BF16)
    kn = kn_ref[0]
    s_old = _dot_nt(q[:, 0:KV_RANK], ck) + _dot_nt(q[:, KV_RANK:D_QK], kp)
    s_new = _dot_nt(q, kn)
    q_pos = past + (lax.broadcasted_iota(jnp.int32, (rows, ts), 0) & (ts - 1))
    k_pos = past + lax.broadcasted_iota(jnp.int32, (rows, ts), 1)
    s_new = jnp.where(_chunk_of(k_pos) <= _chunk_of(q_pos), s_new, -jnp.inf)
    m = jnp.maximum(jnp.max(s_old, axis=1, keepdims=True), jnp.max(s_new, axis=1, keepdims=True))
    p_old = jnp.exp2(s_old - m)
    p_new = jnp.exp2(s_new - m)
    l = jnp.sum(p_old, axis=1, keepdims=True) + jnp.sum(p_new, axis=1, keepdims=True)
    o = (_dot(p_old.astype(BF16), ck) + _dot(p_new.astype(BF16), kn[:, 0:KV_RANK])) / l
    for h in range(N_HEADS):
        o_ref[0, :, KV_RANK * h:KV_RANK * (h + 1)] = o[ts * h:ts * (h + 1), :].astype(BF16)


def _attend_sample(q, cache_ckv, cache_kpe, kcat, layer):
    B, _, ts, _ = q.shape
    past = cache_ckv.shape[2]
    return pl.pallas_call(
        functools.partial(_attend_sample_kernel, ts=ts, past=past),
        grid=(B,),
        in_specs=[
            pl.BlockSpec((1, N_HEADS, ts, D_QK), lambda b: (b, 0, 0, 0)),
            pl.BlockSpec((1, 1, past, KV_RANK), lambda b: (layer, b, 0, 0)),
            pl.BlockSpec((1, 1, past, QK_ROPE), lambda b: (layer, b, 0, 0)),
            pl.BlockSpec((1, ts, D_QK), lambda b: (b, 0, 0)),
        ],
        out_specs=pl.BlockSpec((1, ts, N_HEADS * KV_RANK), lambda b: (b, 0, 0)),
        out_shape=jax.ShapeDtypeStruct((B, ts, N_HEADS * KV_RANK), BF16),
        compiler_params=pltpu.CompilerParams(
            dimension_semantics=("arbitrary",), vmem_limit_bytes=VMEM_LIMIT_BYTES),
        name="attend_sample",
    )(q, cache_ckv, cache_kpe, kcat)


def _post_kernel(x_ref, yconv_ref, olat_ref, fs_ref, wuvg_ref, wo_ref, gpost_ref, gfpre_ref, wup_ref,
                 wfc_ref, bfc_ref, wdown_ref, gfpost_ref,
                 xo_ref, fso_ref,
                 ubuf_ref, vbuf_ref, acc_ref, hf_ref, *, nseq, tt):
    rows = nseq * tt

    @pl.when(pl.program_id(1) == 0)
    def _():
        fso_ref[...] = fs_ref[...]

    half = D_MODEL // 2
    ol = olat_ref[...].reshape(rows, N_HEADS * KV_RANK)
    for g in range(2):
        hf_ref[:, D_CONV + 256 * g:D_CONV + 256 * (g + 1)] = (
            _dot(ol[:, half * g:half * (g + 1)], wuvg_ref[g]).astype(BF16))
    hf_ref[:, 0:D_CONV] = yconv_ref[...].reshape(rows, D_CONV)
    mix = _dot(hf_ref[...], wo_ref[...])
    x1 = x_ref[...].reshape(rows, D_MODEL) + _rms(mix, gpost_ref[...])
    xo_ref[...] = x1.reshape(nseq, tt, D_MODEL)
    hf_ref[...] = _rms(x1, gfpre_ref[...]).astype(BF16)
    acc_ref[...] = jnp.zeros(acc_ref.shape, F32)

    def up_proj(o, w, buf_ref):
        for part in range(2):
            src = slice(D_FF * part + o, D_FF * part + o + w)
            dst = slice(FF_CHUNK * part, FF_CHUNK * part + w)
            buf_ref[:, HALO - 2:HALO, dst] = fso_ref[:, :, src]
            up = _dot(hf_ref[...], wup_ref[:, src]).reshape(nseq, tt, w)
            buf_ref[:, HALO:HALO + tt, dst] = up
            fso_ref[:, :, src] = up[:, tt - 2:tt, :]

    def gate_down(o, w, buf_ref):
        u = []
        for part in range(2):
            src = slice(D_FF * part + o, D_FF * part + o + w)
            dst = slice(FF_CHUNK * part, FF_CHUNK * part + w)
            wc = wfc_ref[:, src]
            u.append((buf_ref[:, HALO - 2:HALO - 2 + tt, dst] * wc[0] + buf_ref[:, HALO - 1:HALO - 1 + tt, dst] * wc[1]
                      + buf_ref[:, HALO:HALO + tt, dst] * wc[2] + bfc_ref[:, src]).reshape(rows, w))
        a, b = u
        gated = (a * (1.0 / (1.0 + jnp.exp(-a))) * b).astype(BF16)
        acc_ref[...] += _dot(gated, wdown_ref[o:o + w, :])

    bufs = (ubuf_ref, vbuf_ref)
    up_proj(*FF_SPLITS[0], bufs[0])
    for c, (o, w) in enumerate(FF_SPLITS):
        if c + 1 < len(FF_SPLITS):
            up_proj(*FF_SPLITS[c + 1], bufs[(c + 1) % 2])
        gate_down(o, w, bufs[c % 2])
    xo_ref[...] = xo_ref[...] + _rms(acc_ref[...], gfpost_ref[...]).reshape(nseq, tt, D_MODEL)


def _post(x, yconv, olat, ffn_state, lw, *, nseq, tt):
    B, T, _ = x.shape
    nb, nt = B // nseq, T // tt
    rows = nseq * tt
    state_block = (nseq, CONV_W - 1, 2 * D_FF)
    in_specs = [
        pl.BlockSpec((nseq, tt, D_MODEL), lambda b, t: (b, t, 0)),
        pl.BlockSpec((nseq, tt, D_CONV), lambda b, t: (b, t, 0)),
        pl.BlockSpec((nseq, tt, N_HEADS * KV_RANK), lambda b, t: (b, t, 0)),
        pl.BlockSpec(state_block, lambda b, t: (b, 0, 0), pipeline_mode=pl.Buffered(1)),
        _const_spec((2, 4 * KV_RANK, 4 * V_HEAD)),
        _const_spec((D_MODEL, D_MODEL)),
        _const_spec((1, D_MODEL)),
        _const_spec((1, D_MODEL)),
        _const_spec((D_MODEL, 2 * D_FF)),
        _const_spec((CONV_W, 2 * D_FF)),
        _const_spec((1, 2 * D_FF)),
        _const_spec((D_FF, D_MODEL)),
        _const_spec((1, D_MODEL)),
    ]
    return pl.pallas_call(
        functools.partial(_post_kernel, nseq=nseq, tt=tt),
        grid=(nb, nt),
        in_specs=in_specs,
        out_specs=(
            pl.BlockSpec((nseq, tt, D_MODEL), lambda b, t: (b, t, 0)),
            pl.BlockSpec(state_block, lambda b, t: (b, 0, 0)),
        ),
        out_shape=(
            jax.ShapeDtypeStruct((B, T, D_MODEL), F32),
            jax.ShapeDtypeStruct((B, CONV_W - 1, 2 * D_FF), F32),
        ),
        scratch_shapes=[
            pltpu.VMEM((nseq, tt + HALO, 2 * FF_CHUNK), F32),
            pltpu.VMEM((nseq, tt + HALO, 2 * FF_CHUNK), F32),
            pltpu.VMEM((rows, D_MODEL), F32),
            pltpu.VMEM((rows, D_MODEL), BF16),
        ],
        compiler_params=pltpu.CompilerParams(
            dimension_semantics=("arbitrary", "arbitrary"), vmem_limit_bytes=VMEM_LIMIT_BYTES),
        name="post_ffn",
    )(x, yconv, olat, ffn_state, lw["w_uvg"], lw["w_o"], lw["g_mix_post"], lw["g_ffn_pre"], lw["w_up"],
      lw["w_ffn_conv"], lw["b_ffn_conv"], lw["w_down"], lw["g_ffn_post"])


def _block_diag(blocks):
    n = len(blocks)
    rows = []
    for i, blk in enumerate(blocks):
        rows.append(jnp.concatenate(
            [blk if j == i else jnp.zeros((blk.shape[0], blocks[j].shape[1]), blk.dtype) for j in range(n)],
            axis=1))
    return jnp.concatenate(rows, axis=0)


def _layer_weights(l, w_in, w_conv, g_qa, w_uq, g_kva, w_uk, w_uv, w_o, g_mix_pre, g_mix_post,
                   w_up, w_ffn_conv, b_ffn_conv, w_down, g_ffn_pre, g_ffn_post):
    d_in = w_in.shape[2]
    uq = w_uq[l].reshape(Q_RANK, N_HEADS, QK_NOPE + QK_ROPE)
    uq = jnp.concatenate([uq[:, :, :QK_NOPE].reshape(Q_RANK, N_HEADS * QK_NOPE),
                          uq[:, :, QK_NOPE:].reshape(Q_RANK, N_HEADS * QK_ROPE)], axis=1)
    uk_t = jnp.transpose(w_uk[l], (1, 2, 0))
    uv = jnp.transpose(w_uv[l], (1, 0, 2))
    return {
        "w1": jnp.pad(w_in[l], ((0, 0), (0, D_IN_PAD - d_in))).astype(BF16),
        "w_conv": w_conv[l],
        "g_mix_pre": g_mix_pre[l][None],
        "g_qa": g_qa[l][None],
        "w_uq": uq.astype(BF16),
        "g_kva": g_kva[l][None],
        "w_ukp": jnp.stack([_block_diag([uk_t[2 * p], uk_t[2 * p + 1]])
                            for p in range(N_HEADS // 2)]).astype(BF16),
        "w_uvg": jnp.stack([_block_diag([uv[4 * g + j] for j in range(4)]) for g in range(2)]).astype(BF16),
        "w_o": w_o[l].astype(BF16),
        "g_mix_post": g_mix_post[l][None],
        "g_ffn_pre": g_ffn_pre[l][None],
        "w_up": w_up[l].astype(BF16),
        "w_ffn_conv": w_ffn_conv[l],
        "b_ffn_conv": b_ffn_conv[l][None],
        "w_down": w_down[l].astype(BF16),
        "g_ffn_post": g_ffn_post[l][None],
    }


def _rope_tables(pos, reps):
    inv = 1.0 / (ROPE_BASE ** (jnp.arange(0, QK_ROPE, 2, dtype=F32) / QK_ROPE))
    ang = pos.astype(F32)[:, None] * inv[None, :]
    cos, sin = jnp.cos(ang), jnp.sin(ang)
    zero = jnp.zeros_like(sin)
    group = 128 // QK_ROPE
    rc = jnp.tile(jnp.concatenate([cos, cos], axis=1), (reps, group))
    rs1 = jnp.tile(jnp.concatenate([-sin, zero], axis=1), (reps, group))
    rs2 = jnp.tile(jnp.concatenate([zero, sin], axis=1), (reps, group))
    return rc, rs1, rs2


def _pick_tile(T, target):
    tt = min(T, target)
    assert T % tt == 0 and tt % 16 == 0
    return tt


def _trunk(x_prompt, x_sample, cache_ckv, cache_kpe, state_conv, state_ffn, weights):
    depth = cache_ckv.shape[0]
    Bp, Tp, _ = x_prompt.shape
    Bs, Ts, _ = x_sample.shape
    past = cache_ckv.shape[2]
    assert Ts >= CONV_W - 1 and Ts % 16 == 0

    tt = _pick_tile(Tp, 512)
    tq = _pick_tile(Tp, 256)
    tk = _pick_tile(Tp, 512)
    ns = math.gcd(Bs, SAMPLE_SEQS_PER_STEP)
    p_ropes = _rope_tables(jnp.arange(Tp), 1)
    s_ropes = _rope_tables(past + jnp.arange(Ts), ns)
    p_conv0 = jnp.zeros((Bp, CONV_W - 1, D_CONV), F32)
    p_ffn0 = jnp.zeros((Bp, CONV_W - 1, 2 * D_FF), F32)

    xp, xs = x_prompt, x_sample
    outs = [[] for _ in range(8)]
    for l in range(depth):
        lw = _layer_weights(l, *weights)
        yc_p, q_p, kcat_p, ckv_p, kpe_p, conv_p, vt_p = _mix_in(
            xp, p_conv0, lw, p_ropes, nseq=1, tt=tt, vt_tile=tk, qt_tile=tq)
        yc_s, q_s, kcat_s, ckv_s, kpe_s, conv_s = _mix_in(xs, state_conv[l], lw, s_ropes, nseq=ns, tt=Ts)
        ol_p = _attend_prompt(q_p, kcat_p, vt_p)
        ol_s = _attend_sample(q_s, cache_ckv, cache_kpe, kcat_s, l)
        xp, ffn_p = _post(xp, yc_p, ol_p, p_ffn0, lw, nseq=1, tt=tt)
        xs, ffn_s = _post(xs, yc_s, ol_s, state_ffn[l], lw, nseq=ns, tt=Ts)
        for acc, v in zip(outs, (ckv_p, kpe_p, conv_p, ffn_p, ckv_s, kpe_s, conv_s, ffn_s)):
            acc.append(v)
    return (xp, xs) + tuple(jnp.stack(v) for v in outs)


def kernel(x_prompt, x_sample, cache_ckv, cache_kpe, state_conv, state_ffn, w_in, w_conv, g_qa, w_uq, g_kva, w_uk, w_uv, w_o, g_mix_pre, g_mix_post, w_up, w_ffn_conv, b_ffn_conv, w_down, g_ffn_pre, g_ffn_post):
    weights = (w_in, w_conv, g_qa, w_uq, g_kva, w_uk, w_uv, w_o, g_mix_pre, g_mix_post,
               w_up, w_ffn_conv, b_ffn_conv, w_down, g_ffn_pre, g_ffn_post)
    return _trunk(x_prompt, x_sample, cache_ckv, cache_kpe, state_conv, state_ffn, weights)
```

```python
import functools
import math

import jax
import jax.numpy as jnp
from jax import lax
from jax.experimental import pallas as pl
from jax.experimental.pallas import tpu as pltpu

D_MODEL = 1024
CHUNK = 64
CONV_W = 3
D_CONV = D_MODEL // 2
N_HEADS = 8
QK_NOPE = 64
QK_ROPE = 32
V_HEAD = 64
KV_RANK = 128
Q_RANK = 256
D_FF = 2816
ROPE_BASE = 10000.0
EPS = 1e-6
ATTN_SCALE = 1.0 / math.sqrt(QK_NOPE + QK_ROPE)
Q_SCALE = ATTN_SCALE * math.log2(math.e)

D_QK = KV_RANK + QK_ROPE
D_IN_PAD = 2048
COL_QA = 3 * D_CONV
COL_KV = COL_QA + Q_RANK
FF_CHUNK = 1408
FF_SPLITS = tuple((o, min(FF_CHUNK, D_FF - o)) for o in range(0, D_FF, FF_CHUNK))
ATTN_UNROLL = 4
ATTN_COL_GROUPS = 1
VT_ROWS = KV_RANK + 16
HALO = 8
SAMPLE_SEQS_PER_STEP = 16

VMEM_LIMIT_BYTES = 56 * 1024 * 1024

BF16 = jnp.bfloat16
F32 = jnp.float32


def _rms(x, g):
    return x * lax.rsqrt(jnp.mean(x * x, axis=-1, keepdims=True) + EPS) * g


def _dot(a, b):
    return jnp.dot(a, b, preferred_element_type=F32)


def _dot_nt(a, b):
    return lax.dot_general(a, b, (((1,), (1,)), ((), ())), preferred_element_type=F32)


def _chunk_of(pos):
    return lax.shift_right_logical(pos, CHUNK.bit_length() - 1)


def _rope(x, rc, rs1, rs2):
    return x * rc + pltpu.roll(x, 128 - QK_ROPE // 2, 1) * rs1 + pltpu.roll(x, QK_ROPE // 2, 1) * rs2


def _causal_conv(buf_ref, cur, prev, w):
    tt = cur.shape[1]
    buf_ref[:, HALO - 2:HALO, :] = prev
    buf_ref[:, HALO:HALO + tt, :] = cur
    s1 = buf_ref[:, HALO - 1:HALO - 1 + tt, :]
    s2 = buf_ref[:, HALO - 2:HALO - 2 + tt, :]
    return s2 * w[0] + s1 * w[1] + cur * w[2]


def _mix_in_kernel(x_ref, cs_ref, w1_ref, wconv_ref, gpre_ref, gqa_ref, wuq_ref, gkva_ref, wukp_ref,
                   rc_ref, rs1_ref, rs2_ref,
                   yconv_ref, q_ref, kcat_ref, ckv_ref, kpe_ref, cso_ref, *rest, nseq, tt, vt_tile, qt_tile):
    cbuf_ref = rest[-1]
    rows = nseq * tt

    @pl.when(pl.program_id(1) == 0)
    def _():
        cso_ref[...] = cs_ref[...]

    x = x_ref[...].reshape(rows, D_MODEL)
    h = _rms(x, gpre_ref[...]).astype(BF16)

    qa = _dot(h, w1_ref[:, COL_QA:COL_KV])
    kv = _dot(h, w1_ref[:, COL_KV:D_IN_PAD])
    zc = _dot(h, w1_ref[:, 0:COL_QA])
    qn = _rms(qa, gqa_ref[...]).astype(BF16)
    q = _dot(qn, wuq_ref[...])
    qls = [_dot(q[:, 128 * p:128 * (p + 1)].astype(BF16), wukp_ref[p]) * Q_SCALE for p in range(N_HEADS // 2)]

    xv = zc[:, 0:D_CONV]
    gb = zc[:, D_CONV:2 * D_CONV]
    gc = zc[:, 2 * D_CONV:3 * D_CONV]
    c = (gc * xv).reshape(nseq, tt, D_CONV)
    conv = _causal_conv(cbuf_ref, c, cso_ref[...], wconv_ref[...])
    yconv_ref[...] = (gb.reshape(nseq, tt, D_CONV) * conv).astype(BF16)
    cso_ref[...] = c[:, tt - 2:tt, :]

    rc = rc_ref[...]
    rs1 = rs1_ref[...]
    rs2 = rs2_ref[...]

    n_nope = N_HEADS * QK_NOPE
    heads_per_tile = 128 // QK_ROPE
    for p in range(N_HEADS // 2):
        ql = qls[p]
        for j in range(2):
            hd = 2 * p + j
            qh = ql[:, KV_RANK * j:KV_RANK * (j + 1)]
            if qt_tile:
                qh_t = qh.T.astype(BF16)
                for i in range(tt // qt_tile):
                    q_ref[0, i, 0:KV_RANK, qt_tile * hd:qt_tile * (hd + 1)] = (
                        qh_t[:, qt_tile * i:qt_tile * (i + 1)])
            else:
                q_ref[:, hd, :, 0:KV_RANK] = qh.reshape(nseq, tt, KV_RANK).astype(BF16)
    for half in range(N_HEADS // heads_per_tile):
        qp = _rope(q[:, n_nope + 128 * half:n_nope + 128 * (half + 1)], rc, rs1, rs2) * Q_SCALE
        qp_t = qp.T.astype(BF16) if qt_tile else None
        for j in range(heads_per_tile):
            hd = heads_per_tile * half + j
            if qt_tile:
                for i in range(tt // qt_tile):
                    q_ref[0, i, KV_RANK:D_QK, qt_tile * hd:qt_tile * (hd + 1)] = (
                        qp_t[QK_ROPE * j:QK_ROPE * (j + 1), qt_tile * i:qt_tile * (i + 1)])
            else:
                q_ref[:, hd, :, KV_RANK:D_QK] = (
                    qp[:, QK_ROPE * j:QK_ROPE * (j + 1)].reshape(nseq, tt, QK_ROPE).astype(BF16))

    ckv = _rms(kv[:, 0:KV_RANK], gkva_ref[...])
    kpe = _rope(kv[:, KV_RANK:2 * KV_RANK], rc, rs1, rs2)[:, 0:QK_ROPE]
    ckv_ref[...] = ckv.reshape(nseq, tt, KV_RANK)
    kpe_ref[...] = kpe.reshape(nseq, tt, QK_ROPE)
    kcat_ref[:, :, 0:KV_RANK] = ckv.reshape(nseq, tt, KV_RANK).astype(BF16)
    kcat_ref[:, :, KV_RANK:D_QK] = kpe.reshape(nseq, tt, QK_ROPE).astype(BF16)
    if vt_tile:
        vt_ref = rest[0]
        for i in range(tt // vt_tile):
            vt_ref[0, i, 0:KV_RANK, :] = ckv[vt_tile * i:vt_tile * (i + 1), :].T.astype(BF16)
            vt_ref[0, i, KV_RANK:VT_ROWS, :] = jnp.ones((VT_ROWS - KV_RANK, vt_tile), BF16)


def _const_spec(shape):
    nd = len(shape)
    return pl.BlockSpec(shape, lambda *_: (0,) * nd, pipeline_mode=pl.Buffered(1))


def _mix_in(x, conv_state, lw, ropes, *, nseq, tt, vt_tile=0, qt_tile=0):
    B, T, _ = x.shape
    assert not vt_tile or (nseq == 1 and tt % vt_tile == 0)
    assert not qt_tile or (nseq == 1 and tt % qt_tile == 0)
    if qt_tile:
        q_shape = jax.ShapeDtypeStruct((B, T // qt_tile, D_QK, N_HEADS * qt_tile), BF16)
        q_spec = pl.BlockSpec((1, tt // qt_tile, D_QK, N_HEADS * qt_tile), lambda b, t: (b, t, 0, 0))
    else:
        q_shape = jax.ShapeDtypeStruct((B, N_HEADS, T, D_QK), BF16)
        q_spec = pl.BlockSpec((nseq, N_HEADS, tt, D_QK), lambda b, t: (b, 0, t, 0))
    nb, nt = B // nseq, T // tt
    rows = nseq * tt
    rope_spec = pl.BlockSpec((rows, 128), lambda b, t: (t, 0))
    in_specs = [
        pl.BlockSpec((nseq, tt, D_MODEL), lambda b, t: (b, t, 0)),
        pl.BlockSpec((nseq, CONV_W - 1, D_CONV), lambda b, t: (b, 0, 0)),
        _const_spec((D_MODEL, D_IN_PAD)),
        _const_spec((CONV_W, D_CONV)),
        _const_spec((1, D_MODEL)),
        _const_spec((1, Q_RANK)),
        _const_spec((Q_RANK, N_HEADS * (QK_NOPE + QK_ROPE))),
        _const_spec((1, KV_RANK)),
        _const_spec((N_HEADS // 2, 128, 256)),
        rope_spec, rope_spec, rope_spec,
    ]
    out_shape = [
        jax.ShapeDtypeStruct((B, T, D_CONV), BF16),
        q_shape,
        jax.ShapeDtypeStruct((B, T, D_QK), BF16),
        jax.ShapeDtypeStruct((B, T, KV_RANK), F32),
        jax.ShapeDtypeStruct((B, T, QK_ROPE), F32),
        jax.ShapeDtypeStruct((B, CONV_W - 1, D_CONV), F32),
    ]
    out_specs = [
        pl.BlockSpec((nseq, tt, D_CONV), lambda b, t: (b, t, 0)),
        q_spec,
        pl.BlockSpec((nseq, tt, D_QK), lambda b, t: (b, t, 0)),
        pl.BlockSpec((nseq, tt, KV_RANK), lambda b, t: (b, t, 0)),
        pl.BlockSpec((nseq, tt, QK_ROPE), lambda b, t: (b, t, 0)),
        pl.BlockSpec((nseq, CONV_W - 1, D_CONV), lambda b, t: (b, 0, 0)),
    ]
    if vt_tile:
        out_shape.append(jax.ShapeDtypeStruct((B, T // vt_tile, VT_ROWS, vt_tile), BF16))
        out_specs.append(pl.BlockSpec((1, tt // vt_tile, VT_ROWS, vt_tile), lambda b, t: (b, t, 0, 0)))
    return pl.pallas_call(
        functools.partial(_mix_in_kernel, nseq=nseq, tt=tt, vt_tile=vt_tile, qt_tile=qt_tile),
        grid=(nb, nt),
        in_specs=in_specs,
        out_specs=out_specs,
        out_shape=out_shape,
        scratch_shapes=[pltpu.VMEM((nseq, tt + HALO, D_CONV), F32)],
        compiler_params=pltpu.CompilerParams(
            dimension_semantics=("arbitrary", "arbitrary"), vmem_limit_bytes=VMEM_LIMIT_BYTES),
        name="mix_in",
    )(x, conv_state, lw["w1"], lw["w_conv"], lw["g_mix_pre"], lw["g_qa"], lw["w_uq"], lw["g_kva"],
      lw["w_ukp"], *ropes)


def _attend_prompt_kernel(q_ref, k_ref, vt_ref, o_ref, m_ref, acc_ref, sa_ref, sb_ref, sa_max_ref, sb_max_ref,
                          *, tq, tk):
    cols = N_HEADS * tq
    gw = cols // ATTN_COL_GROUPS
    q0 = pl.program_id(1) * tq
    n_last = q0 // tk

    m_ref[...] = jnp.full(m_ref.shape, -jnp.inf, F32)
    acc_ref[...] = jnp.zeros(acc_ref.shape, F32)

    def scores(j, buf):
        s_ref, smax_ref = buf
        k0 = pl.multiple_of(j * tk, tk)
        s = _dot(k_ref[0, pl.ds(k0, tk), :], q_ref[0, 0])
        s_ref[...] = s
        smax_ref[...] = jnp.max(s, axis=0, keepdims=True)

    def consume(j, buf, masked):
        s_ref, smax_ref = buf
        vt = vt_ref[0, j]
        for g in range(ATTN_COL_GROUPS):
            sl = slice(gw * g, gw * (g + 1))
            s = s_ref[:, sl]
            if masked:
                k_pos = j * tk + lax.broadcasted_iota(jnp.int32, (tk, gw), 0)
                q_pos = q0 + (lax.broadcasted_iota(jnp.int32, (tk, gw), 1) & (tq - 1))
                s = jnp.where(_chunk_of(k_pos) <= _chunk_of(q_pos), s, -jnp.inf)
                s_max = jnp.max(s, axis=0, keepdims=True)
            else:
                s_max = smax_ref[:, sl]
            m_prev = m_ref[:, sl]
            m_new = jnp.maximum(m_prev, s_max)
            p = jnp.exp2(s - m_new).astype(BF16)
            acc_ref[:, sl] = jnp.exp2(m_prev - m_new) * acc_ref[:, sl] + _dot(vt, p)
            m_ref[:, sl] = m_new

    bufs = ((sa_ref, sa_max_ref), (sb_ref, sb_max_ref))

    def run(first, count):
        for t in range(count):
            scores(first + t + 1, bufs[(t + 1) % 2])
            consume(first + t, bufs[t % 2], False)

    scores(0, bufs[0])

    def unrolled_body(i, carry):
        run(ATTN_UNROLL * i, ATTN_UNROLL)
        return carry

    lax.fori_loop(0, n_last // ATTN_UNROLL, unrolled_body, 0)
    done = (n_last // ATTN_UNROLL) * ATTN_UNROLL
    for rest in range(ATTN_UNROLL):
        @pl.when(n_last - done == rest)
        def _():
            run(done, rest)
            consume(n_last, bufs[rest % 2], True)

    o = acc_ref[0:KV_RANK, :] / acc_ref[KV_RANK:KV_RANK + 1, :]
    for h in range(N_HEADS):
        o_ref[0, :, KV_RANK * h:KV_RANK * (h + 1)] = o[:, tq * h:tq * (h + 1)].T.astype(BF16)


def _attend_prompt(q_t, kcat, vt):
    B, T, _ = kcat.shape
    tk = vt.shape[3]
    cols = q_t.shape[3]
    tq = cols // N_HEADS
    assert tq % CHUNK == 0 and tq & (tq - 1) == 0 and T % tk == 0 and tk % tq == 0
    return pl.pallas_call(
        functools.partial(_attend_prompt_kernel, tq=tq, tk=tk),
        grid=(B, T // tq),
        in_specs=[
            pl.BlockSpec((1, 1, D_QK, cols), lambda b, i: (b, i, 0, 0)),
            pl.BlockSpec((1, T, D_QK), lambda b, i: (b, 0, 0)),
            pl.BlockSpec((1, T // tk, VT_ROWS, tk), lambda b, i: (b, 0, 0, 0)),
        ],
        out_specs=pl.BlockSpec((1, tq, N_HEADS * KV_RANK), lambda b, i: (b, i, 0)),
        out_shape=jax.ShapeDtypeStruct((B, T, N_HEADS * KV_RANK), BF16),
        scratch_shapes=[
            pltpu.VMEM((1, cols), F32),
            pltpu.VMEM((VT_ROWS, cols), F32),
            pltpu.VMEM((tk, cols), F32),
            pltpu.VMEM((tk, cols), F32),
            pltpu.VMEM((1, cols), F32),
            pltpu.VMEM((1, cols), F32),
        ],
        compiler_params=pltpu.CompilerParams(
            dimension_semantics=("arbitrary", "arbitrary"), vmem_limit_bytes=VMEM_LIMIT_BYTES),
        name="attend_prompt",
    )(q_t, kcat, vt)


def _attend_sample_kernel(q_ref, ckv_ref, kpe_ref, kn_ref, o_ref, *, ts, past):
    rows = N_HEADS * ts
    q = q_ref[0].reshape(rows, D_QK)
    ck = ckv_ref[0, 0].astype(BF16)
    kp = kpe_ref[0, 0].astype(BF16)
    kn = kn_ref[0]
    s_old = _dot_nt(q[:, 0:KV_RANK], ck) + _dot_nt(q[:, KV_RANK:D_QK], kp)
    s_new = _dot_nt(q, kn)
    q_pos = past + (lax.broadcasted_iota(jnp.int32, (rows, ts), 0) & (ts - 1))
    k_pos = past + lax.broadcasted_iota(jnp.int32, (rows, ts), 1)
    s_new = jnp.where(_chunk_of(k_pos) <= _chunk_of(q_pos), s_new, -jnp.inf)
    m = jnp.maximum(jnp.max(s_old, axis=1, keepdims=True), jnp.max(s_new, axis=1, keepdims=True))
    p_old = jnp.exp2(s_old - m)
    p_new = jnp.exp2(s_new - m)
    l = jnp.sum(p_old, axis=1, keepdims=True) + jnp.sum(p_new, axis=1, keepdims=True)
    o = (_dot(p_old.astype(BF16), ck) + _dot(p_new.astype(BF16), kn[:, 0:KV_RANK])) / l
    for h in range(N_HEADS):
        o_ref[0, :, KV_RANK * h:KV_RANK * (h + 1)] = o[ts * h:ts * (h + 1), :].astype(BF16)


def _attend_sample(q, cache_ckv, cache_kpe, kcat, layer):
    B, _, ts, _ = q.shape
    past = cache_ckv.shape[2]
    return pl.pallas_call(
        functools.partial(_attend_sample_kernel, ts=ts, past=past),
        grid=(B,),
        in_specs=[
            pl.BlockSpec((1, N_HEADS, ts, D_QK), lambda b: (b, 0, 0, 0)),
            pl.BlockSpec((1, 1, past, KV_RANK), lambda b: (layer, b, 0, 0)),
            pl.BlockSpec((1, 1, past, QK_ROPE), lambda b: (layer, b, 0, 0)),
            pl.BlockSpec((1, ts, D_QK), lambda b: (b, 0, 0)),
        ],
        out_specs=pl.BlockSpec((1, ts, N_HEADS * KV_RANK), lambda b: (b, 0, 0)),
        out_shape=jax.ShapeDtypeStruct((B, ts, N_HEADS * KV_RANK), BF16),
        compiler_params=pltpu.CompilerParams(
            dimension_semantics=("arbitrary",), vmem_limit_bytes=VMEM_LIMIT_BYTES),
        name="attend_sample",
    )(q, cache_ckv, cache_kpe, kcat)


def _post_kernel(x_ref, yconv_ref, olat_ref, fs_ref, wuvg_ref, wo_ref, gpost_ref, gfpre_ref, wup_ref,
                 wfc_ref, bfc_ref, wdown_ref, gfpost_ref,
                 xo_ref, fso_ref,
                 ubuf_ref, vbuf_ref, acc_ref, hf_ref, *, nseq, tt):
    rows = nseq * tt

    @pl.when(pl.program_id(1) == 0)
    def _():
        fso_ref[...] = fs_ref[...]

    half = D_MODEL // 2
    ol = olat_ref[...].reshape(rows, N_HEADS * KV_RANK)
    for g in range(2):
        hf_ref[:, D_CONV + 256 * g:D_CONV + 256 * (g + 1)] = (
            _dot(ol[:, half * g:half * (g + 1)], wuvg_ref[g]).astype(BF16))
    hf_ref[:, 0:D_CONV] = yconv_ref[...].reshape(rows, D_CONV)
    mix = _dot(hf_ref[...], wo_ref[...])
    x1 = x_ref[...].reshape(rows, D_MODEL) + _rms(mix, gpost_ref[...])
    xo_ref[...] = x1.reshape(nseq, tt, D_MODEL)
    hf_ref[...] = _rms(x1, gfpre_ref[...]).astype(BF16)
    acc_ref[...] = jnp.zeros(acc_ref.shape, F32)

    def up_proj(o, w, buf_ref):
        for part in range(2):
            src = slice(D_FF * part + o, D_FF * part + o + w)
            dst = slice(FF_CHUNK * part, FF_CHUNK * part + w)
            buf_ref[:, HALO - 2:HALO, dst] = fso_ref[:, :, src]
            up = _dot(hf_ref[...], wup_ref[:, src]).reshape(nseq, tt, w)
            buf_ref[:, HALO:HALO + tt, dst] = up
            fso_ref[:, :, src] = up[:, tt - 2:tt, :]

    def gate_down(o, w, buf_ref):
        u = []
        for part in range(2):
            src = slice(D_FF * part + o, D_FF * part + o + w)
            dst = slice(FF_CHUNK * part, FF_CHUNK * part + w)
            wc = wfc_ref[:, src]
            u.append((buf_ref[:, HALO - 2:HALO - 2 + tt, dst] * wc[0] + buf_ref[:, HALO - 1:HALO - 1 + tt, dst] * wc[1]
                      + buf_ref[:, HALO:HALO + tt, dst] * wc[2] + bfc_ref[:, src]).reshape(rows, w))
        a, b = u
        gated = (a * (1.0 / (1.0 + jnp.exp(-a))) * b).astype(BF16)
        acc_ref[...] += _dot(gated, wdown_ref[o:o + w, :])

    bufs = (ubuf_ref, vbuf_ref)
    up_proj(*FF_SPLITS[0], bufs[0])
    for c, (o, w) in enumerate(FF_SPLITS):
        if c + 1 < len(FF_SPLITS):
            up_proj(*FF_SPLITS[c + 1], bufs[(c + 1) % 2])
        gate_down(o, w, bufs[c % 2])
    xo_ref[...] = xo_ref[...] + _rms(acc_ref[...], gfpost_ref[...]).reshape(nseq, tt, D_MODEL)


def _post(x, yconv, olat, ffn_state, lw, *, nseq, tt):
    B, T, _ = x.shape
    nb, nt = B // nseq, T // tt
    rows = nseq * tt
    state_block = (nseq, CONV_W - 1, 2 * D_FF)
    in_specs = [
        pl.BlockSpec((nseq, tt, D_MODEL), lambda b, t: (b, t, 0)),
        pl.BlockSpec((nseq, tt, D_CONV), lambda b, t: (b, t, 0)),
        pl.BlockSpec((nseq, tt, N_HEADS * KV_RANK), lambda b, t: (b, t, 0)),
        pl.BlockSpec(state_block, lambda b, t: (b, 0, 0), pipeline_mode=pl.Buffered(1)),
        _const_spec((2, 4 * KV_RANK, 4 * V_HEAD)),
        _const_spec((D_MODEL, D_MODEL)),
        _const_spec((1, D_MODEL)),
        _const_spec((1, D_MODEL)),
        _const_spec((D_MODEL, 2 * D_FF)),
        _const_spec((CONV_W, 2 * D_FF)),
        _const_spec((1, 2 * D_FF)),
        _const_spec((D_FF, D_MODEL)),
        _const_spec((1, D_MODEL)),
    ]
    return pl.pallas_call(
        functools.partial(_post_kernel, nseq=nseq, tt=tt),
        grid=(nb, nt),
        in_specs=in_specs,
        out_specs=(
            pl.BlockSpec((nseq, tt, D_MODEL), lambda b, t: (b, t, 0)),
            pl.BlockSpec(state_block, lambda b, t: (b, 0, 0)),
        ),
        out_shape=(
            jax.ShapeDtypeStruct((B, T, D_MODEL), F32),
            jax.ShapeDtypeStruct((B, CONV_W - 1, 2 * D_FF), F32),
        ),
        scratch_shapes=[
            pltpu.VMEM((nseq, tt + HALO, 2 * FF_CHUNK), F32),
            pltpu.VMEM((nseq, tt + HALO, 2 * FF_CHUNK), F32),
            pltpu.VMEM((rows, D_MODEL), F32),
            pltpu.VMEM((rows, D_MODEL), BF16),
        ],
        compiler_params=pltpu.CompilerParams(
            dimension_semantics=("arbitrary", "arbitrary"), vmem_limit_bytes=VMEM_LIMIT_BYTES),
        name="post_ffn",
    )(x, yconv, olat, ffn_state, lw["w_uvg"], lw["w_o"], lw["g_mix_post"], lw["g_ffn_pre"], lw["w_up"],
      lw["w_ffn_conv"], lw["b_ffn_conv"], lw["w_down"], lw["g_ffn_post"])


def _block_diag(blocks):
    n = len(blocks)
    rows = []
    for i, blk in enumerate(blocks):
        rows.append(jnp.concatenate(
            [blk if j == i else jnp.zeros((blk.shape[0], blocks[j].shape[1]), blk.dtype) for j in range(n)],
            axis=1))
    return jnp.concatenate(rows, axis=0)


def _layer_weights(l, w_in, w_conv, g_qa, w_uq, g_kva, w_uk, w_uv, w_o, g_mix_pre, g_mix_post,
                   w_up, w_ffn_conv, b_ffn_conv, w_down, g_ffn_pre, g_ffn_post):
    d_in = w_in.shape[2]
    uq = w_uq[l].reshape(Q_RANK, N_HEADS, QK_NOPE + QK_ROPE)
    uq = jnp.concatenate([uq[:, :, :QK_NOPE].reshape(Q_RANK, N_HEADS * QK_NOPE),
                          uq[:, :, QK_NOPE:].reshape(Q_RANK, N_HEADS * QK_ROPE)], axis=1)
    uk_t = jnp.transpose(w_uk[l], (1, 2, 0))
    uv = jnp.transpose(w_uv[l], (1, 0, 2))
    return {
        "w1": jnp.pad(w_in[l], ((0, 0), (0, D_IN_PAD - d_in))).astype(BF16),
        "w_conv": w_conv[l],
        "g_mix_pre": g_mix_pre[l][None],
        "g_qa": g_qa[l][None],
        "w_uq": uq.astype(BF16),
        "g_kva": g_kva[l][None],
        "w_ukp": jnp.stack([_block_diag([uk_t[2 * p], uk_t[2 * p + 1]])
                            for p in range(N_HEADS // 2)]).astype(BF16),
        "w_uvg": jnp.stack([_block_diag([uv[4 * g + j] for j in range(4)]) for g in range(2)]).astype(BF16),
        "w_o": w_o[l].astype(BF16),
        "g_mix_post": g_mix_post[l][None],
        "g_ffn_pre": g_ffn_pre[l][None],
        "w_up": w_up[l].astype(BF16),
        "w_ffn_conv": w_ffn_conv[l],
        "b_ffn_conv": b_ffn_conv[l][None],
        "w_down": w_down[l].astype(BF16),
        "g_ffn_post": g_ffn_post[l][None],
    }


def _rope_tables(pos, reps):
    inv = 1.0 / (ROPE_BASE ** (jnp.arange(0, QK_ROPE, 2, dtype=F32) / QK_ROPE))
    ang = pos.astype(F32)[:, None] * inv[None, :]
    cos, sin = jnp.cos(ang), jnp.sin(ang)
    zero = jnp.zeros_like(sin)
    group = 128 // QK_ROPE
    rc = jnp.tile(jnp.concatenate([cos, cos], axis=1), (reps, group))
    rs1 = jnp.tile(jnp.concatenate([-sin, zero], axis=1), (reps, group))
    rs2 = jnp.tile(jnp.concatenate([zero, sin], axis=1), (reps, group))
    return rc, rs1, rs2


def _pick_tile(T, target):
    tt = min(T, target)
    assert T % tt == 0 and tt % 16 == 0
    return tt


def _trunk(x_prompt, x_sample, cache_ckv, cache_kpe, state_conv, state_ffn, weights):
    depth = cache_ckv.shape[0]
    Bp, Tp, _ = x_prompt.shape
    Bs, Ts, _ = x_sample.shape
    past = cache_ckv.shape[2]
    assert Ts >= CONV_W - 1 and Ts % 16 == 0

    tt = _pick_tile(Tp, 512)
    tq = _pick_tile(Tp, 256)
    tk = _pick_tile(Tp, 512)
    ns = math.gcd(Bs, SAMPLE_SEQS_PER_STEP)
    p_ropes = _rope_tables(jnp.arange(Tp), 1)
    s_ropes = _rope_tables(past + jnp.arange(Ts), ns)
    p_conv0 = jnp.zeros((Bp, CONV_W - 1, D_CONV), F32)
    p_ffn0 = jnp.zeros((Bp, CONV_W - 1, 2 * D_FF), F32)

    xp, xs = x_prompt, x_sample
    outs = [[] for _ in range(8)]
    for l in range(depth):
        lw = _layer_weights(l, *weights)
        yc_p, q_p, kcat_p, ckv_p, kpe_p, conv_p, vt_p = _mix_in(
            xp, p_conv0, lw, p_ropes, nseq=1, tt=tt, vt_tile=tk, qt_tile=tq)
        yc_s, q_s, kcat_s, ckv_s, kpe_s, conv_s = _mix_in(xs, state_conv[l], lw, s_ropes, nseq=ns, tt=Ts)
        ol_p = _attend_prompt(q_p, kcat_p, vt_p)
        ol_s = _attend_sample(q_s, cache_ckv, cache_kpe, kcat_s, l)
        xp, ffn_p = _post(xp, yc_p, ol_p, p_ffn0, lw, nseq=1, tt=tt)
        xs, ffn_s = _post(xs, yc_s, ol_s, state_ffn[l], lw, nseq=ns, tt=Ts)
        for acc, v in zip(outs, (ckv_p, kpe_p, conv_p, ffn_p, ckv_s, kpe_s, conv_s, ffn_s)):
            acc.append(v)
    return (xp, xs) + tuple(jnp.stack(v) for v in outs)


def kernel(x_prompt, x_sample, cache_ckv, cache_kpe, state_conv, state_ffn, w_in, w_conv, g_qa, w_uq, g_kva, w_uk, w_uv, w_o, g_mix_pre, g_mix_post, w_up, w_ffn_conv, b_ffn_conv, w_down, g_ffn_pre, g_ffn_post):
    weights = (w_in, w_conv, g_qa, w_uq, g_kva, w_uk, w_uv, w_o, g_mix_pre, g_mix_post,
               w_up, w_ffn_conv, b_ffn_conv, w_down, g_ffn_pre, g_ffn_post)
    return _trunk(x_prompt, x_sample, cache_ckv, cache_kpe, state_conv, state_ffn, weights)
```

```python
import functools
import math

import jax
import jax.numpy as jnp
from jax import lax
from jax.experimental import pallas as pl
from jax.experimental.pallas import tpu as pltpu

D_MODEL = 1024
CHUNK = 64
CONV_W = 3
D_CONV = D_MODEL // 2
N_HEADS = 8
QK_NOPE = 64
QK_ROPE = 32
V_HEAD = 64
KV_RANK = 128
Q_RANK = 256
D_FF = 2816
ROPE_BASE = 10000.0
EPS = 1e-6
ATTN_SCALE = 1.0 / math.sqrt(QK_NOPE + QK_ROPE)
Q_SCALE = ATTN_SCALE * math.log2(math.e)

D_QK = KV_RANK + QK_ROPE
D_IN_PAD = 2048
COL_QA = 3 * D_CONV
COL_KV = COL_QA + Q_RANK
FF_SPLITS = ((0, 1536), (1536, 1280))
FF_CHUNK = max(w for _, w in FF_SPLITS)
assert sum(w for _, w in FF_SPLITS) == D_FF and all(o % 128 == 0 for o, _ in FF_SPLITS)
ATTN_UNROLL = 4
ATTN_COL_GROUPS = 1
VT_ROWS = KV_RANK + 16
HALO = 8
SAMPLE_SEQS_PER_STEP = 16

VMEM_LIMIT_BYTES = 56 * 1024 * 1024

BF16 = jnp.bfloat16
F32 = jnp.float32


def _rms(x, g):
    return x * lax.rsqrt(jnp.mean(x * x, axis=-1, keepdims=True) + EPS) * g


def _dot(a, b):
    return jnp.dot(a, b, preferred_element_type=F32)


def _dot_nt(a, b):
    return lax.dot_general(a, b, (((1,), (1,)), ((), ())), preferred_element_type=F32)


def _chunk_of(pos):
    return lax.shift_right_logical(pos, CHUNK.bit_length() - 1)


def _rope(x, rc, rs1, rs2):
    return x * rc + pltpu.roll(x, 128 - QK_ROPE // 2, 1) * rs1 + pltpu.roll(x, QK_ROPE // 2, 1) * rs2


def _causal_conv(buf_ref, cur, prev, w):
    tt = cur.shape[1]
    buf_ref[:, HALO - 2:HALO, :] = prev
    buf_ref[:, HALO:HALO + tt, :] = cur
    s1 = buf_ref[:, HALO - 1:HALO - 1 + tt, :]
    s2 = buf_ref[:, HALO - 2:HALO - 2 + tt, :]
    return s2 * w[0] + s1 * w[1] + cur * w[2]


def _mix_in_kernel(x_ref, cs_ref, w1_ref, wconv_ref, gpre_ref, gqa_ref, wuq_ref, gkva_ref, wukp_ref,
                   rc_ref, rs1_ref, rs2_ref,
                   yconv_ref, q_ref, kcat_ref, ckv_ref, kpe_ref, cso_ref, *rest, nseq, tt, vt_tile, qt_tile):
    cbuf_ref = rest[-1]
    rows = nseq * tt

    @pl.when(pl.program_id(1) == 0)
    def _():
        cso_ref[...] = cs_ref[...]

    x = x_ref[...].reshape(rows, D_MODEL)
    h = _rms(x, gpre_ref[...]).astype(BF16)

    qa = _dot(h, w1_ref[:, COL_QA:COL_KV])
    kv = _dot(h, w1_ref[:, COL_KV:D_IN_PAD])
    zc = _dot(h, w1_ref[:, 0:COL_QA])
    qn = _rms(qa, gqa_ref[...]).astype(BF16)
    q = _dot(qn, wuq_ref[...])
    qls = [_dot(q[:, 128 * p:128 * (p + 1)].astype(BF16), wukp_ref[p]) * Q_SCALE for p in range(N_HEADS // 2)]

    xv = zc[:, 0:D_CONV]
    gb = zc[:, D_CONV:2 * D_CONV]
    gc = zc[:, 2 * D_CONV:3 * D_CONV]
    c = (gc * xv).reshape(nseq, tt, D_CONV)
    conv = _causal_conv(cbuf_ref, c, cso_ref[...], wconv_ref[...])
    yconv_ref[...] = (gb.reshape(nseq, tt, D_CONV) * conv).astype(BF16)
    cso_ref[...] = c[:, tt - 2:tt, :]

    rc = rc_ref[...]
    rs1 = rs1_ref[...]
    rs2 = rs2_ref[...]

    n_nope = N_HEADS * QK_NOPE
    heads_per_tile = 128 // QK_ROPE
    for p in range(N_HEADS // 2):
        ql = qls[p]
        for j in range(2):
            hd = 2 * p + j
            qh = ql[:, KV_RANK * j:KV_RANK * (j + 1)]
            if qt_tile:
                qh_t = qh.T.astype(BF16)
                for i in range(tt // qt_tile):
                    q_ref[0, i, 0:KV_RANK, qt_tile * hd:qt_tile * (hd + 1)] = (
                        qh_t[:, qt_tile * i:qt_tile * (i + 1)])
            else:
                q_ref[:, hd, :, 0:KV_RANK] = qh.reshape(nseq, tt, KV_RANK).astype(BF16)
    for half in range(N_HEADS // heads_per_tile):
        qp = _rope(q[:, n_nope + 128 * half:n_nope + 128 * (half + 1)], rc, rs1, rs2) * Q_SCALE
        qp_t = qp.T.astype(BF16) if qt_tile else None
        for j in range(heads_per_tile):
            hd = heads_per_tile * half + j
            if qt_tile:
                for i in range(tt // qt_tile):
                    q_ref[0, i, KV_RANK:D_QK, qt_tile * hd:qt_tile * (hd + 1)] = (
                        qp_t[QK_ROPE * j:QK_ROPE * (j + 1), qt_tile * i:qt_tile * (i + 1)])
            else:
                q_ref[:, hd, :, KV_RANK:D_QK] = (
                    qp[:, QK_ROPE * j:QK_ROPE * (j + 1)].reshape(nseq, tt, QK_ROPE).astype(BF16))

    ckv = _rms(kv[:, 0:KV_RANK], gkva_ref[...])
    kpe = _rope(kv[:, KV_RANK:2 * KV_RANK], rc, rs1, rs2)[:, 0:QK_ROPE]
    ckv_ref[...] = ckv.reshape(nseq, tt, KV_RANK)
    kpe_ref[...] = kpe.reshape(nseq, tt, QK_ROPE)
    kcat_ref[:, :, 0:KV_RANK] = ckv.reshape(nseq, tt, KV_RANK).astype(BF16)
    kcat_ref[:, :, KV_RANK:D_QK] = kpe.reshape(nseq, tt, QK_ROPE).astype(BF16)
    if vt_tile:
        vt_ref = rest[0]
        for i in range(tt // vt_tile):
            vt_ref[0, i, 0:KV_RANK, :] = ckv[vt_tile * i:vt_tile * (i + 1), :].T.astype(BF16)
            vt_ref[0, i, KV_RANK:VT_ROWS, :] = jnp.ones((VT_ROWS - KV_RANK, vt_tile), BF16)


def _const_spec(shape):
    nd = len(shape)
    return pl.BlockSpec(shape, lambda *_: (0,) * nd, pipeline_mode=pl.Buffered(1))


def _mix_in(x, conv_state, lw, ropes, *, nseq, tt, vt_tile=0, qt_tile=0):
    B, T, _ = x.shape
    assert not vt_tile or (nseq == 1 and tt % vt_tile == 0)
    assert not qt_tile or (nseq == 1 and tt % qt_tile == 0)
    if qt_tile:
        q_shape = jax.ShapeDtypeStruct((B, T // qt_tile, D_QK, N_HEADS * qt_tile), BF16)
        q_spec = pl.BlockSpec((1, tt // qt_tile, D_QK, N_HEADS * qt_tile), lambda b, t: (b, t, 0, 0))
    else:
        q_shape = jax.ShapeDtypeStruct((B, N_HEADS, T, D_QK), BF16)
        q_spec = pl.BlockSpec((nseq, N_HEADS, tt, D_QK), lambda b, t: (b, 0, t, 0))
    nb, nt = B // nseq, T // tt
    rows = nseq * tt
    rope_spec = pl.BlockSpec((rows, 128), lambda b, t: (t, 0))
    in_specs = [
        pl.BlockSpec((nseq, tt, D_MODEL), lambda b, t: (b, t, 0)),
        pl.BlockSpec((nseq, CONV_W - 1, D_CONV), lambda b, t: (b, 0, 0)),
        _const_spec((D_MODEL, D_IN_PAD)),
        _const_spec((CONV_W, D_CONV)),
        _const_spec((1, D_MODEL)),
        _const_spec((1, Q_RANK)),
        _const_spec((Q_RANK, N_HEADS * (QK_NOPE + QK_ROPE))),
        _const_spec((1, KV_RANK)),
        _const_spec((N_HEADS // 2, 128, 256)),
        rope_spec, rope_spec, rope_spec,
    ]
    out_shape = [
        jax.ShapeDtypeStruct((B, T, D_CONV), BF16),
        q_shape,
        jax.ShapeDtypeStruct((B, T, D_QK), BF16),
        jax.ShapeDtypeStruct((B, T, KV_RANK), F32),
        jax.ShapeDtypeStruct((B, T, QK_ROPE), F32),
        jax.ShapeDtypeStruct((B, CONV_W - 1, D_CONV), F32),
    ]
    out_specs = [
        pl.BlockSpec((nseq, tt, D_CONV), lambda b, t: (b, t, 0)),
        q_spec,
        pl.BlockSpec((nseq, tt, D_QK), lambda b, t: (b, t, 0)),
        pl.BlockSpec((nseq, tt, KV_RANK), lambda b, t: (b, t, 0)),
        pl.BlockSpec((nseq, tt, QK_ROPE), lambda b, t: (b, t, 0)),
        pl.BlockSpec((nseq, CONV_W - 1, D_CONV), lambda b, t: (b, 0, 0)),
    ]
    if vt_tile:
        out_shape.append(jax.ShapeDtypeStruct((B, T // vt_tile, VT_ROWS, vt_tile), BF16))
        out_specs.append(pl.BlockSpec((1, tt // vt_tile, VT_ROWS, vt_tile), lambda b, t: (b, t, 0, 0)))
    return pl.pallas_call(
        functools.partial(_mix_in_kernel, nseq=nseq, tt=tt, vt_tile=vt_tile, qt_tile=qt_tile),
        grid=(nb, nt),
        in_specs=in_specs,
        out_specs=out_specs,
        out_shape=out_shape,
        scratch_shapes=[pltpu.VMEM((nseq, tt + HALO, D_CONV), F32)],
        compiler_params=pltpu.CompilerParams(
            dimension_semantics=("arbitrary", "arbitrary"), vmem_limit_bytes=VMEM_LIMIT_BYTES),
        name="mix_in",
    )(x, conv_state, lw["w1"], lw["w_conv"], lw["g_mix_pre"], lw["g_qa"], lw["w_uq"], lw["g_kva"],
      lw["w_ukp"], *ropes)


def _attend_prompt_kernel(q_ref, k_ref, vt_ref, o_ref, m_ref, acc_ref, sa_ref, sb_ref, sa_max_ref, sb_max_ref,
                          *, tq, tk):
    cols = N_HEADS * tq
    gw = cols // ATTN_COL_GROUPS
    q0 = pl.program_id(1) * tq
    n_last = q0 // tk

    m_ref[...] = jnp.full(m_ref.shape, -jnp.inf, F32)
    acc_ref[...] = jnp.zeros(acc_ref.shape, F32)

    def scores(j, buf):
        s_ref, smax_ref = buf
        k0 = pl.multiple_of(j * tk, tk)
        s = _dot(k_ref[0, pl.ds(k0, tk), :], q_ref[0, 0])
        s_ref[...] = s
        smax_ref[...] = jnp.max(s, axis=0, keepdims=True)

    def consume(j, buf, masked):
        s_ref, smax_ref = buf
        vt = vt_ref[0, j]
        for g in range(ATTN_COL_GROUPS):
            sl = slice(gw * g, gw * (g + 1))
            s = s_ref[:, sl]
            if masked:
                k_pos = j * tk + lax.broadcasted_iota(jnp.int32, (tk, gw), 0)
                q_pos = q0 + (lax.broadcasted_iota(jnp.int32, (tk, gw), 1) & (tq - 1))
                s = jnp.where(_chunk_of(k_pos) <= _chunk_of(q_pos), s, -jnp.inf)
                s_max = jnp.max(s, axis=0, keepdims=True)
            else:
                s_max = smax_ref[:, sl]
            m_prev = m_ref[:, sl]
            m_new = jnp.maximum(m_prev, s_max)
            p = jnp.exp2(s - m_new).astype(BF16)
            acc_ref[:, sl] = jnp.exp2(m_prev - m_new) * acc_ref[:, sl] + _dot(vt, p)
            m_ref[:, sl] = m_new

    bufs = ((sa_ref, sa_max_ref), (sb_ref, sb_max_ref))

    def run(first, count):
        for t in range(count):
            scores(first + t + 1, bufs[(t + 1) % 2])
            consume(first + t, bufs[t % 2], False)

    scores(0, bufs[0])

    def unrolled_body(i, carry):
        run(ATTN_UNROLL * i, ATTN_UNROLL)
        return carry

    lax.fori_loop(0, n_last // ATTN_UNROLL, unrolled_body, 0)
    done = (n_last // ATTN_UNROLL) * ATTN_UNROLL
    for rest in range(ATTN_UNROLL):
        @pl.when(n_last - done == rest)
        def _():
            run(done, rest)
            consume(n_last, bufs[rest % 2], True)

    o = acc_ref[0:KV_RANK, :] / acc_ref[KV_RANK:KV_RANK + 1, :]
    for h in range(N_HEADS):
        o_ref[0, :, KV_RANK * h:KV_RANK * (h + 1)] = o[:, tq * h:tq * (h + 1)].T.astype(BF16)


def _attend_prompt(q_t, kcat, vt):
    B, T, _ = kcat.shape
    tk = vt.shape[3]
    cols = q_t.shape[3]
    tq = cols // N_HEADS
    assert tq % CHUNK == 0 and tq & (tq - 1) == 0 and T % tk == 0 and tk % tq == 0
    return pl.pallas_call(
        functools.partial(_attend_prompt_kernel, tq=tq, tk=tk),
        grid=(B, T // tq),
        in_specs=[
            pl.BlockSpec((1, 1, D_QK, cols), lambda b, i: (b, i, 0, 0)),
            pl.BlockSpec((1, T, D_QK), lambda b, i: (b, 0, 0)),
            pl.BlockSpec((1, T // tk, VT_ROWS, tk), lambda b, i: (b, 0, 0, 0)),
        ],
        out_specs=pl.BlockSpec((1, tq, N_HEADS * KV_RANK), lambda b, i: (b, i, 0)),
        out_shape=jax.ShapeDtypeStruct((B, T, N_HEADS * KV_RANK), BF16),
        scratch_shapes=[
            pltpu.VMEM((1, cols), F32),
            pltpu.VMEM((VT_ROWS, cols), F32),
            pltpu.VMEM((tk, cols), F32),
            pltpu.VMEM((tk, cols), F32),
            pltpu.VMEM((1, cols), F32),
            pltpu.VMEM((1, cols), F32),
        ],
        compiler_params=pltpu.CompilerParams(
            dimension_semantics=("arbitrary", "arbitrary"), vmem_limit_bytes=VMEM_LIMIT_BYTES),
        name="attend_prompt",
    )(q_t, kcat, vt)


def _attend_sample_kernel(q_ref, ckv_ref, kpe_ref, kn_ref, o_ref, *, ts, past):
    rows = N_HEADS * ts
    q = q_ref[0].reshape(rows, D_QK)
    ck = ckv_ref[0, 0].astype(BF16)
    kp_t = kpe_ref[0, 0].astype(BF16)
    kn = kn_ref[0]
    s_old = _dot_nt(q[:, 0:KV_RANK], ck) + _dot(q[:, KV_RANK:D_QK], kp_t)
    s_new = _dot_nt(q, kn)
    q_pos = past + (lax.broadcasted_iota(jnp.int32, (rows, ts), 0) & (ts - 1))
    k_pos = past + lax.broadcasted_iota(jnp.int32, (rows, ts), 1)
    s_new = jnp.where(_chunk_of(k_pos) <= _chunk_of(q_pos), s_new, -jnp.inf)
    m = jnp.maximum(jnp.max(s_old, axis=1, keepdims=True), jnp.max(s_new, axis=1, keepdims=True))
    p_old = jnp.exp2(s_old - m)
    p_new = jnp.exp2(s_new - m)
    l = jnp.sum(p_old, axis=1, keepdims=True) + jnp.sum(p_new, axis=1, keepdims=True)
    o = (_dot(p_old.astype(BF16), ck) + _dot(p_new.astype(BF16), kn[:, 0:KV_RANK])) / l
    for h in range(N_HEADS):
        o_ref[0, :, KV_RANK * h:KV_RANK * (h + 1)] = o[ts * h:ts * (h + 1), :].astype(BF16)


def _attend_sample(q, cache_ckv, cache_kpe_t, kcat, layer):
    B, _, ts, _ = q.shape
    past = cache_ckv.shape[2]
    return pl.pallas_call(
        functools.partial(_attend_sample_kernel, ts=ts, past=past),
        grid=(B,),
        in_specs=[
            pl.BlockSpec((1, N_HEADS, ts, D_QK), lambda b: (b, 0, 0, 0)),
            pl.BlockSpec((1, 1, past, KV_RANK), lambda b: (layer, b, 0, 0)),
            pl.BlockSpec((1, 1, QK_ROPE, past), lambda b: (layer, b, 0, 0)),
            pl.BlockSpec((1, ts, D_QK), lambda b: (b, 0, 0)),
        ],
        out_specs=pl.BlockSpec((1, ts, N_HEADS * KV_RANK), lambda b: (b, 0, 0)),
        out_shape=jax.ShapeDtypeStruct((B, ts, N_HEADS * KV_RANK), BF16),
        compiler_params=pltpu.CompilerParams(
            dimension_semantics=("arbitrary",), vmem_limit_bytes=VMEM_LIMIT_BYTES),
        name="attend_sample",
    )(q, cache_ckv, cache_kpe_t, kcat)


def _post_kernel(x_ref, yconv_ref, olat_ref, fs_ref, wuvg_ref, wo_ref, gpost_ref, gfpre_ref, wup_ref,
                 wfc_ref, bfc_ref, wdown_ref, gfpost_ref,
                 xo_ref, fso_ref,
                 ubuf_ref, vbuf_ref, acc_ref, hf_ref, *, nseq, tt):
    rows = nseq * tt

    @pl.when(pl.program_id(1) == 0)
    def _():
        fso_ref[...] = fs_ref[...]

    half = D_MODEL // 2
    ol = olat_ref[...].reshape(rows, N_HEADS * KV_RANK)
    for g in range(2):
        hf_ref[:, D_CONV + 256 * g:D_CONV + 256 * (g + 1)] = (
            _dot(ol[:, half * g:half * (g + 1)], wuvg_ref[g]).astype(BF16))
    hf_ref[:, 0:D_CONV] = yconv_ref[...].reshape(rows, D_CONV)
    mix = _dot(hf_ref[...], wo_ref[...])
    x1 = x_ref[...].reshape(rows, D_MODEL) + _rms(mix, gpost_ref[...])
    xo_ref[...] = x1.reshape(nseq, tt, D_MODEL)
    hf_ref[...] = _rms(x1, gfpre_ref[...]).astype(BF16)
    acc_ref[...] = jnp.zeros(acc_ref.shape, F32)

    def up_proj(o, w, buf_ref):
        for part in range(2):
            src = slice(D_FF * part + o, D_FF * part + o + w)
            dst = slice(FF_CHUNK * part, FF_CHUNK * part + w)
            buf_ref[:, HALO - 2:HALO, dst] = fso_ref[:, :, src]
            up = _dot(hf_ref[...], wup_ref[:, src]).reshape(nseq, tt, w)
            buf_ref[:, HALO:HALO + tt, dst] = up
            fso_ref[:, :, src] = up[:, tt - 2:tt, :]

    def gate_down(o, w, buf_ref):
        u = []
        for part in range(2):
            src = slice(D_FF * part + o, D_FF * part + o + w)
            dst = slice(FF_CHUNK * part, FF_CHUNK * part + w)
            wc = wfc_ref[:, src]
            u.append((buf_ref[:, HALO - 2:HALO - 2 + tt, dst] * wc[0] + buf_ref[:, HALO - 1:HALO - 1 + tt, dst] * wc[1]
                      + buf_ref[:, HALO:HALO + tt, dst] * wc[2] + bfc_ref[:, src]).reshape(rows, w))
        a, b = u
        gated = (a * (1.0 / (1.0 + jnp.exp(-a))) * b).astype(BF16)
        acc_ref[...] += _dot(gated, wdown_ref[o:o + w, :])

    bufs = (ubuf_ref, vbuf_ref)
    up_proj(*FF_SPLITS[0], bufs[0])
    for c, (o, w) in enumerate(FF_SPLITS):
        if c + 1 < len(FF_SPLITS):
            up_proj(*FF_SPLITS[c + 1], bufs[(c + 1) % 2])
        gate_down(o, w, bufs[c % 2])
    xo_ref[...] = xo_ref[...] + _rms(acc_ref[...], gfpost_ref[...]).reshape(nseq, tt, D_MODEL)


def _post(x, yconv, olat, ffn_state, lw, *, nseq, tt):
    B, T, _ = x.shape
    nb, nt = B // nseq, T // tt
    rows = nseq * tt
    state_block = (nseq, CONV_W - 1, 2 * D_FF)
    in_specs = [
        pl.BlockSpec((nseq, tt, D_MODEL), lambda b, t: (b, t, 0)),
        pl.BlockSpec((nseq, tt, D_CONV), lambda b, t: (b, t, 0)),
        pl.BlockSpec((nseq, tt, N_HEADS * KV_RANK), lambda b, t: (b, t, 0)),
        pl.BlockSpec(state_block, lambda b, t: (b, 0, 0), pipeline_mode=pl.Buffered(1)),
        _const_spec((2, 4 * KV_RANK, 4 * V_HEAD)),
        _const_spec((D_MODEL, D_MODEL)),
        _const_spec((1, D_MODEL)),
        _const_spec((1, D_MODEL)),
        _const_spec((D_MODEL, 2 * D_FF)),
        _const_spec((CONV_W, 2 * D_FF)),
        _const_spec((1, 2 * D_FF)),
        _const_spec((D_FF, D_MODEL)),
        _const_spec((1, D_MODEL)),
    ]
    return pl.pallas_call(
        functools.partial(_post_kernel, nseq=nseq, tt=tt),
        grid=(nb, nt),
        in_specs=in_specs,
        out_specs=(
            pl.BlockSpec((nseq, tt, D_MODEL), lambda b, t: (b, t, 0)),
            pl.BlockSpec(state_block, lambda b, t: (b, 0, 0)),
        ),
        out_shape=(
            jax.ShapeDtypeStruct((B, T, D_MODEL), F32),
            jax.ShapeDtypeStruct((B, CONV_W - 1, 2 * D_FF), F32),
        ),
        scratch_shapes=[
            pltpu.VMEM((nseq, tt + HALO, 2 * FF_CHUNK), F32),
            pltpu.VMEM((nseq, tt + HALO, 2 * FF_CHUNK), F32),
            pltpu.VMEM((rows, D_MODEL), F32),
            pltpu.VMEM((rows, D_MODEL), BF16),
        ],
        compiler_params=pltpu.CompilerParams(
            dimension_semantics=("arbitrary", "arbitrary"), vmem_limit_bytes=VMEM_LIMIT_BYTES),
        name="post_ffn",
    )(x, yconv, olat, ffn_state, lw["w_uvg"], lw["w_o"], lw["g_mix_post"], lw["g_ffn_pre"], lw["w_up"],
      lw["w_ffn_conv"], lw["b_ffn_conv"], lw["w_down"], lw["g_ffn_post"])


def _block_diag(blocks):
    n = len(blocks)
    rows = []
    for i, blk in enumerate(blocks):
        rows.append(jnp.concatenate(
            [blk if j == i else jnp.zeros((blk.shape[0], blocks[j].shape[1]), blk.dtype) for j in range(n)],
            axis=1))
    return jnp.concatenate(rows, axis=0)


def _layer_weights(l, w_in, w_conv, g_qa, w_uq, g_kva, w_uk, w_uv, w_o, g_mix_pre, g_mix_post,
                   w_up, w_ffn_conv, b_ffn_conv, w_down, g_ffn_pre, g_ffn_post):
    d_in = w_in.shape[2]
    uq = w_uq[l].reshape(Q_RANK, N_HEADS, QK_NOPE + QK_ROPE)
    uq = jnp.concatenate([uq[:, :, :QK_NOPE].reshape(Q_RANK, N_HEADS * QK_NOPE),
                          uq[:, :, QK_NOPE:].reshape(Q_RANK, N_HEADS * QK_ROPE)], axis=1)
    uk_t = jnp.transpose(w_uk[l], (1, 2, 0))
    uv = jnp.transpose(w_uv[l], (1, 0, 2))
    return {
        "w1": jnp.pad(w_in[l], ((0, 0), (0, D_IN_PAD - d_in))).astype(BF16),
        "w_conv": w_conv[l],
        "g_mix_pre": g_mix_pre[l][None],
        "g_qa": g_qa[l][None],
        "w_uq": uq.astype(BF16),
        "g_kva": g_kva[l][None],
        "w_ukp": jnp.stack([_block_diag([uk_t[2 * p], uk_t[2 * p + 1]])
                            for p in range(N_HEADS // 2)]).astype(BF16),
        "w_uvg": jnp.stack([_block_diag([uv[4 * g + j] for j in range(4)]) for g in range(2)]).astype(BF16),
        "w_o": w_o[l].astype(BF16),
        "g_mix_post": g_mix_post[l][None],
        "g_ffn_pre": g_ffn_pre[l][None],
        "w_up": w_up[l].astype(BF16),
        "w_ffn_conv": w_ffn_conv[l],
        "b_ffn_conv": b_ffn_conv[l][None],
        "w_down": w_down[l].astype(BF16),
        "g_ffn_post": g_ffn_post[l][None],
    }


def _rope_tables(pos, reps):
    inv = 1.0 / (ROPE_BASE ** (jnp.arange(0, QK_ROPE, 2, dtype=F32) / QK_ROPE))
    ang = pos.astype(F32)[:, None] * inv[None, :]
    cos, sin = jnp.cos(ang), jnp.sin(ang)
    zero = jnp.zeros_like(sin)
    group = 128 // QK_ROPE
    rc = jnp.tile(jnp.concatenate([cos, cos], axis=1), (reps, group))
    rs1 = jnp.tile(jnp.concatenate([-sin, zero], axis=1), (reps, group))
    rs2 = jnp.tile(jnp.concatenate([zero, sin], axis=1), (reps, group))
    return rc, rs1, rs2


def _pick_tile(T, target):
    tt = min(T, target)
    assert T % tt == 0 and tt % 16 == 0
    return tt


def _trunk(x_prompt, x_sample, cache_ckv, cache_kpe, state_conv, state_ffn, weights):
    depth = cache_ckv.shape[0]
    Bp, Tp, _ = x_prompt.shape
    Bs, Ts, _ = x_sample.shape
    past = cache_ckv.shape[2]
    assert Ts >= CONV_W - 1 and Ts % 16 == 0

    tt = _pick_tile(Tp, 512)
    tq = _pick_tile(Tp, 256)
    tk = _pick_tile(Tp, 512)
    ns = math.gcd(Bs, SAMPLE_SEQS_PER_STEP)
    p_ropes = _rope_tables(jnp.arange(Tp), 1)
    s_ropes = _rope_tables(past + jnp.arange(Ts), ns)
    p_conv0 = jnp.zeros((Bp, CONV_W - 1, D_CONV), F32)
    p_ffn0 = jnp.zeros((Bp, CONV_W - 1, 2 * D_FF), F32)

    cache_kpe_t = jnp.swapaxes(cache_kpe, 2, 3)
    xp, xs = x_prompt, x_sample
    outs = [[] for _ in range(8)]
    for l in range(depth):
        lw = _layer_weights(l, *weights)
        yc_p, q_p, kcat_p, ckv_p, kpe_p, conv_p, vt_p = _mix_in(
            xp, p_conv0, lw, p_ropes, nseq=1, tt=tt, vt_tile=tk, qt_tile=tq)
        yc_s, q_s, kcat_s, ckv_s, kpe_s, conv_s = _mix_in(xs, state_conv[l], lw, s_ropes, nseq=ns, tt=Ts)
        ol_p = _attend_prompt(q_p, kcat_p, vt_p)
        ol_s = _attend_sample(q_s, cache_ckv, cache_kpe_t, kcat_s, l)
        xp, ffn_p = _post(xp, yc_p, ol_p, p_ffn0, lw, nseq=1, tt=tt)
        xs, ffn_s = _post(xs, yc_s, ol_s, state_ffn[l], lw, nseq=ns, tt=Ts)
        for acc, v in zip(outs, (ckv_p, kpe_p, conv_p, ffn_p, ckv_s, kpe_s, conv_s, ffn_s)):
            acc.append(v)
    return (xp, xs) + tuple(jnp.stack(v) for v in outs)


def kernel(x_prompt, x_sample, cache_ckv, cache_kpe, state_conv, state_ffn, w_in, w_conv, g_qa, w_uq, g_kva, w_uk, w_uv, w_o, g_mix_pre, g_mix_post, w_up, w_ffn_conv, b_ffn_conv, w_down, g_ffn_pre, g_ffn_post):
    weights = (w_in, w_conv, g_qa, w_uq, g_kva, w_uk, w_uv, w_o, g_mix_pre, g_mix_post,
               w_up, w_ffn_conv, b_ffn_conv, w_down, g_ffn_pre, g_ffn_post)
    return _trunk(x_prompt, x_sample, cache_ckv, cache_kpe, state_conv, state_ffn, weights)
```

```python
import functools
import math

import jax
import jax.numpy as jnp
from jax import lax
from jax.experimental import pallas as pl
from jax.experimental.pallas import tpu as pltpu

D_MODEL = 1024
CHUNK = 64
CONV_W = 3
D_CONV = D_MODEL // 2
N_HEADS = 8
QK_NOPE = 64
QK_ROPE = 32
V_HEAD = 64
KV_RANK = 128
Q_RANK = 256
D_FF = 2816
ROPE_BASE = 10000.0
EPS = 1e-6
ATTN_SCALE = 1.0 / math.sqrt(QK_NOPE + QK_ROPE)
Q_SCALE = ATTN_SCALE * math.log2(math.e)

D_QK = KV_RANK + QK_ROPE
D_IN_PAD = 2048
COL_QA = 3 * D_CONV
COL_KV = COL_QA + Q_RANK
FF_SPLITS = ((0, 1536), (1536, 1280))
FF_CHUNK = max(w for _, w in FF_SPLITS)
assert sum(w for _, w in FF_SPLITS) == D_FF and all(o % 128 == 0 for o, _ in FF_SPLITS)
ATTN_UNROLL = 4
ATTN_COL_GROUPS = 1
VT_ROWS = KV_RANK + 16
HALO = 8
SAMPLE_SEQS_PER_STEP = 16

VMEM_LIMIT_BYTES = 56 * 1024 * 1024

BF16 = jnp.bfloat16
F32 = jnp.float32


def _rms(x, g):
    return x * lax.rsqrt(jnp.mean(x * x, axis=-1, keepdims=True) + EPS) * g


def _dot(a, b):
    return jnp.dot(a, b, preferred_element_type=F32)


def _dot_nt(a, b):
    return lax.dot_general(a, b, (((1,), (1,)), ((), ())), preferred_element_type=F32)


def _chunk_of(pos):
    return lax.shift_right_logical(pos, CHUNK.bit_length() - 1)


def _rope(x, rc, rs1, rs2):
    return x * rc + pltpu.roll(x, 128 - QK_ROPE // 2, 1) * rs1 + pltpu.roll(x, QK_ROPE // 2, 1) * rs2


def _causal_conv(buf_ref, cur, prev, w):
    tt = cur.shape[1]
    buf_ref[:, HALO - 2:HALO, :] = prev
    buf_ref[:, HALO:HALO + tt, :] = cur
    s1 = buf_ref[:, HALO - 1:HALO - 1 + tt, :]
    s2 = buf_ref[:, HALO - 2:HALO - 2 + tt, :]
    return s2 * w[0] + s1 * w[1] + cur * w[2]


def _mix_in_kernel(x_ref, cs_ref, w1_ref, wconv_ref, gpre_ref, gqa_ref, wuq_ref, gkva_ref, wukp_ref,
                   rc_ref, rs1_ref, rs2_ref,
                   yconv_ref, q_ref, kcat_ref, ckv_ref, kpe_ref, cso_ref, *rest, nseq, tt, vt_tile, qt_tile):
    cbuf_ref = rest[-1]
    rows = nseq * tt

    @pl.when(pl.program_id(1) == 0)
    def _():
        cso_ref[...] = cs_ref[...]

    x = x_ref[...].reshape(rows, D_MODEL)
    h = _rms(x, gpre_ref[...]).astype(BF16)

    qa = _dot(h, w1_ref[:, COL_QA:COL_KV])
    kv = _dot(h, w1_ref[:, COL_KV:D_IN_PAD])
    xv = _dot(h, w1_ref[:, 0:D_CONV])
    gc = _dot(h, w1_ref[:, 2 * D_CONV:3 * D_CONV])
    gb = _dot(h, w1_ref[:, D_CONV:2 * D_CONV])
    qn = _rms(qa, gqa_ref[...]).astype(BF16)
    q = _dot(qn, wuq_ref[...])
    qls = [_dot(q[:, 128 * p:128 * (p + 1)].astype(BF16), wukp_ref[p]) * Q_SCALE for p in range(N_HEADS // 2)]

    c = (gc * xv).reshape(nseq, tt, D_CONV)
    conv = _causal_conv(cbuf_ref, c, cso_ref[...], wconv_ref[...])
    yconv_ref[...] = (gb.reshape(nseq, tt, D_CONV) * conv).astype(BF16)
    cso_ref[...] = c[:, tt - 2:tt, :]

    rc = rc_ref[...]
    rs1 = rs1_ref[...]
    rs2 = rs2_ref[...]

    n_nope = N_HEADS * QK_NOPE
    heads_per_tile = 128 // QK_ROPE
    for p in range(N_HEADS // 2):
        ql = qls[p]
        for j in range(2):
            hd = 2 * p + j
            qh = ql[:, KV_RANK * j:KV_RANK * (j + 1)]
            if qt_tile:
                qh_t = qh.T.astype(BF16)
                for i in range(tt // qt_tile):
                    q_ref[0, i, 0:KV_RANK, qt_tile * hd:qt_tile * (hd + 1)] = (
                        qh_t[:, qt_tile * i:qt_tile * (i + 1)])
            else:
                q_ref[:, hd, :, 0:KV_RANK] = qh.reshape(nseq, tt, KV_RANK).astype(BF16)
    for half in range(N_HEADS // heads_per_tile):
        qp = _rope(q[:, n_nope + 128 * half:n_nope + 128 * (half + 1)], rc, rs1, rs2) * Q_SCALE
        qp_t = qp.T.astype(BF16) if qt_tile else None
        for j in range(heads_per_tile):
            hd = heads_per_tile * half + j
            if qt_tile:
                for i in range(tt // qt_tile):
                    q_ref[0, i, KV_RANK:D_QK, qt_tile * hd:qt_tile * (hd + 1)] = (
                        qp_t[QK_ROPE * j:QK_ROPE * (j + 1), qt_tile * i:qt_tile * (i + 1)])
            else:
                q_ref[:, hd, :, KV_RANK:D_QK] = (
                    qp[:, QK_ROPE * j:QK_ROPE * (j + 1)].reshape(nseq, tt, QK_ROPE).astype(BF16))

    ckv = _rms(kv[:, 0:KV_RANK], gkva_ref[...])
    kpe = _rope(kv[:, KV_RANK:2 * KV_RANK], rc, rs1, rs2)[:, 0:QK_ROPE]
    ckv_ref[...] = ckv.reshape(nseq, tt, KV_RANK)
    kpe_ref[...] = kpe.reshape(nseq, tt, QK_ROPE)
    kcat_ref[:, :, 0:KV_RANK] = ckv.reshape(nseq, tt, KV_RANK).astype(BF16)
    kcat_ref[:, :, KV_RANK:D_QK] = kpe.reshape(nseq, tt, QK_ROPE).astype(BF16)
    if vt_tile:
        vt_ref = rest[0]
        for i in range(tt // vt_tile):
            vt_ref[0, i, 0:KV_RANK, :] = ckv[vt_tile * i:vt_tile * (i + 1), :].T.astype(BF16)
            vt_ref[0, i, KV_RANK:VT_ROWS, :] = jnp.ones((VT_ROWS - KV_RANK, vt_tile), BF16)


def _const_spec(shape):
    nd = len(shape)
    return pl.BlockSpec(shape, lambda *_: (0,) * nd, pipeline_mode=pl.Buffered(1))


def _mix_in(x, conv_state, lw, ropes, *, nseq, tt, vt_tile=0, qt_tile=0):
    B, T, _ = x.shape
    assert not vt_tile or (nseq == 1 and tt % vt_tile == 0)
    assert not qt_tile or (nseq == 1 and tt % qt_tile == 0)
    if qt_tile:
        q_shape = jax.ShapeDtypeStruct((B, T // qt_tile, D_QK, N_HEADS * qt_tile), BF16)
        q_spec = pl.BlockSpec((1, tt // qt_tile, D_QK, N_HEADS * qt_tile), lambda b, t: (b, t, 0, 0))
    else:
        q_shape = jax.ShapeDtypeStruct((B, N_HEADS, T, D_QK), BF16)
        q_spec = pl.BlockSpec((nseq, N_HEADS, tt, D_QK), lambda b, t: (b, 0, t, 0))
    nb, nt = B // nseq, T // tt
    rows = nseq * tt
    rope_spec = pl.BlockSpec((rows, 128), lambda b, t: (t, 0))
    in_specs = [
        pl.BlockSpec((nseq, tt, D_MODEL), lambda b, t: (b, t, 0)),
        pl.BlockSpec((nseq, CONV_W - 1, D_CONV), lambda b, t: (b, 0, 0)),
        _const_spec((D_MODEL, D_IN_PAD)),
        _const_spec((CONV_W, D_CONV)),
        _const_spec((1, D_MODEL)),
        _const_spec((1, Q_RANK)),
        _const_spec((Q_RANK, N_HEADS * (QK_NOPE + QK_ROPE))),
        _const_spec((1, KV_RANK)),
        _const_spec((N_HEADS // 2, 128, 256)),
        rope_spec, rope_spec, rope_spec,
    ]
    out_shape = [
        jax.ShapeDtypeStruct((B, T, D_CONV), BF16),
        q_shape,
        jax.ShapeDtypeStruct((B, T, D_QK), BF16),
        jax.ShapeDtypeStruct((B, T, KV_RANK), F32),
        jax.ShapeDtypeStruct((B, T, QK_ROPE), F32),
        jax.ShapeDtypeStruct((B, CONV_W - 1, D_CONV), F32),
    ]
    out_specs = [
        pl.BlockSpec((nseq, tt, D_CONV), lambda b, t: (b, t, 0)),
        q_spec,
        pl.BlockSpec((nseq, tt, D_QK), lambda b, t: (b, t, 0)),
        pl.BlockSpec((nseq, tt, KV_RANK), lambda b, t: (b, t, 0)),
        pl.BlockSpec((nseq, tt, QK_ROPE), lambda b, t: (b, t, 0)),
        pl.BlockSpec((nseq, CONV_W - 1, D_CONV), lambda b, t: (b, 0, 0)),
    ]
    if vt_tile:
        out_shape.append(jax.ShapeDtypeStruct((B, T // vt_tile, VT_ROWS, vt_tile), BF16))
        out_specs.append(pl.BlockSpec((1, tt // vt_tile, VT_ROWS, vt_tile), lambda b, t: (b, t, 0, 0)))
    return pl.pallas_call(
        functools.partial(_mix_in_kernel, nseq=nseq, tt=tt, vt_tile=vt_tile, qt_tile=qt_tile),
        grid=(nb, nt),
        in_specs=in_specs,
        out_specs=out_specs,
        out_shape=out_shape,
        scratch_shapes=[pltpu.VMEM((nseq, tt + HALO, D_CONV), F32)],
        compiler_params=pltpu.CompilerParams(
            dimension_semantics=("arbitrary", "arbitrary"), vmem_limit_bytes=VMEM_LIMIT_BYTES),
        name="mix_in",
    )(x, conv_state, lw["w1"], lw["w_conv"], lw["g_mix_pre"], lw["g_qa"], lw["w_uq"], lw["g_kva"],
      lw["w_ukp"], *ropes)


def _attend_prompt_kernel(q_ref, qn_ref, k_ref, vt_ref, o_ref, m_ref, acc_ref,
                          s0_ref, sa_ref, sb_ref, s0_max_ref, sa_max_ref, sb_max_ref, *, tq, tk):
    cols = N_HEADS * tq
    gw = cols // ATTN_COL_GROUPS
    qi = pl.program_id(1)
    q0 = qi * tq
    n_last = q0 // tk

    m_ref[...] = jnp.full(m_ref.shape, -jnp.inf, F32)
    acc_ref[...] = jnp.zeros(acc_ref.shape, F32)

    def scores(j, buf, query_ref=q_ref):
        s_ref, smax_ref = buf
        k0 = pl.multiple_of(j * tk, tk)
        s = _dot(k_ref[0, pl.ds(k0, tk), :], query_ref[0, 0])
        s_ref[...] = s
        smax_ref[...] = jnp.max(s, axis=0, keepdims=True)

    def consume(j, buf, masked):
        s_ref, smax_ref = buf
        vt = vt_ref[0, j]
        for g in range(ATTN_COL_GROUPS):
            sl = slice(gw * g, gw * (g + 1))
            s = s_ref[:, sl]
            if masked:
                k_pos = j * tk + lax.broadcasted_iota(jnp.int32, (tk, gw), 0)
                q_pos = q0 + (lax.broadcasted_iota(jnp.int32, (tk, gw), 1) & (tq - 1))
                s = jnp.where(_chunk_of(k_pos) <= _chunk_of(q_pos), s, -jnp.inf)
                s_max = jnp.max(s, axis=0, keepdims=True)
            else:
                s_max = smax_ref[:, sl]
            m_prev = m_ref[:, sl]
            m_new = jnp.maximum(m_prev, s_max)
            p = jnp.exp2(s - m_new).astype(BF16)
            acc_ref[:, sl] = jnp.exp2(m_prev - m_new) * acc_ref[:, sl] + _dot(vt, p)
            m_ref[:, sl] = m_new

    first = (s0_ref, s0_max_ref)
    bufs = ((sa_ref, sa_max_ref), (sb_ref, sb_max_ref))

    def score_next_query_tile():
        scores(0, first, qn_ref)

    def run(start, count):
        for t in range(count):
            scores(start + t + 1, bufs[(t + 1) % 2])
            consume(start + t, bufs[t % 2], False)

    @pl.when(qi == 0)
    def _():
        scores(0, first)

    @pl.when(n_last == 0)
    def _():
        consume(0, first, True)
        score_next_query_tile()

    @pl.when(n_last > 0)
    def _():
        scores(1, bufs[0])
        consume(0, first, False)

    later = jnp.maximum(n_last - 1, 0)

    def unrolled_body(i, carry):
        run(1 + ATTN_UNROLL * i, ATTN_UNROLL)
        return carry

    lax.fori_loop(0, later // ATTN_UNROLL, unrolled_body, 0)
    done = (later // ATTN_UNROLL) * ATTN_UNROLL
    for rest in range(ATTN_UNROLL):
        @pl.when((n_last > 0) & (later - done == rest))
        def _():
            run(1 + done, rest)
            score_next_query_tile()
            consume(n_last, bufs[rest % 2], True)

    o = acc_ref[0:KV_RANK, :] / acc_ref[KV_RANK:KV_RANK + 1, :]
    for h in range(N_HEADS):
        o_ref[0, :, KV_RANK * h:KV_RANK * (h + 1)] = o[:, tq * h:tq * (h + 1)].T.astype(BF16)


def _attend_prompt(q_t, kcat, vt):
    B, T, _ = kcat.shape
    tk = vt.shape[3]
    cols = q_t.shape[3]
    tq = cols // N_HEADS
    nq = T // tq
    assert tq % CHUNK == 0 and tq & (tq - 1) == 0 and T % tk == 0 and tk % tq == 0
    score_buf = pltpu.VMEM((tk, cols), F32)
    max_buf = pltpu.VMEM((1, cols), F32)
    return pl.pallas_call(
        functools.partial(_attend_prompt_kernel, tq=tq, tk=tk),
        grid=(B, nq),
        in_specs=[
            pl.BlockSpec((1, 1, D_QK, cols), lambda b, i: (b, i, 0, 0)),
            pl.BlockSpec((1, 1, D_QK, cols), lambda b, i: (b, jnp.minimum(i + 1, nq - 1), 0, 0)),
            pl.BlockSpec((1, T, D_QK), lambda b, i: (b, 0, 0)),
            pl.BlockSpec((1, T // tk, VT_ROWS, tk), lambda b, i: (b, 0, 0, 0)),
        ],
        out_specs=pl.BlockSpec((1, tq, N_HEADS * KV_RANK), lambda b, i: (b, i, 0)),
        out_shape=jax.ShapeDtypeStruct((B, T, N_HEADS * KV_RANK), BF16),
        scratch_shapes=[
            pltpu.VMEM((1, cols), F32),
            pltpu.VMEM((VT_ROWS, cols), F32),
            score_buf, score_buf, score_buf,
            max_buf, max_buf, max_buf,
        ],
        compiler_params=pltpu.CompilerParams(
            dimension_semantics=("arbitrary", "arbitrary"), vmem_limit_bytes=VMEM_LIMIT_BYTES),
        name="attend_prompt",
    )(q_t, q_t, kcat, vt)


def _attend_sample_kernel(q_ref, ckv_ref, kpe_ref, kn_ref, o_ref, *, ts, past):
    rows = N_HEADS * ts
    q = q_ref[0].reshape(rows, D_QK)
    ck = ckv_ref[0, 0].astype(BF16)
    kp_t = kpe_ref[0, 0].astype(BF16)
    kn = kn_ref[0]
    s_old = _dot_nt(q[:, 0:KV_RANK], ck) + _dot(q[:, KV_RANK:D_QK], kp_t)
    s_new = _dot_nt(q, kn)
    q_pos = past + (lax.broadcasted_iota(jnp.int32, (rows, ts), 0) & (ts - 1))
    k_pos = past + lax.broadcasted_iota(jnp.int32, (rows, ts), 1)
    s_new = jnp.where(_chunk_of(k_pos) <= _chunk_of(q_pos), s_new, -jnp.inf)
    m = jnp.maximum(jnp.max(s_old, axis=1, keepdims=True), jnp.max(s_new, axis=1, keepdims=True))
    p_old = jnp.exp2(s_old - m)
    p_new = jnp.exp2(s_new - m)
    l = jnp.sum(p_old, axis=1, keepdims=True) + jnp.sum(p_new, axis=1, keepdims=True)
    o = (_dot(p_old.astype(BF16), ck) + _dot(p_new.astype(BF16), kn[:, 0:KV_RANK])) / l
    for h in range(N_HEADS):
        o_ref[0, :, KV_RANK * h:KV_RANK * (h + 1)] = o[ts * h:ts * (h + 1), :].astype(BF16)


def _attend_sample(q, cache_ckv, cache_kpe_t, kcat, layer):
    B, _, ts, _ = q.shape
    past = cache_ckv.shape[2]
    return pl.pallas_call(
        functools.partial(_attend_sample_kernel, ts=ts, past=past),
        grid=(B,),
        in_specs=[
            pl.BlockSpec((1, N_HEADS, ts, D_QK), lambda b: (b, 0, 0, 0)),
            pl.BlockSpec((1, 1, past, KV_RANK), lambda b: (layer, b, 0, 0)),
            pl.BlockSpec((1, 1, QK_ROPE, past), lambda b: (layer, b, 0, 0)),
            pl.BlockSpec((1, ts, D_QK), lambda b: (b, 0, 0)),
        ],
        out_specs=pl.BlockSpec((1, ts, N_HEADS * KV_RANK), lambda b: (b, 0, 0)),
        out_shape=jax.ShapeDtypeStruct((B, ts, N_HEADS * KV_RANK), BF16),
        compiler_params=pltpu.CompilerParams(
            dimension_semantics=("arbitrary",), vmem_limit_bytes=VMEM_LIMIT_BYTES),
        name="attend_sample",
    )(q, cache_ckv, cache_kpe_t, kcat)


def _post_kernel(x_ref, yconv_ref, olat_ref, fs_ref, wuvg_ref, wo_ref, gpost_ref, gfpre_ref, wup_ref,
                 wfc_ref, bfc_ref, wdown_ref, gfpost_ref,
                 xo_ref, fso_ref,
                 ubuf_ref, vbuf_ref, acc_ref, hf_ref, *, nseq, tt):
    rows = nseq * tt

    @pl.when(pl.program_id(1) == 0)
    def _():
        fso_ref[...] = fs_ref[...]

    half = D_MODEL // 2
    ol = olat_ref[...].reshape(rows, N_HEADS * KV_RANK)
    for g in range(2):
        hf_ref[:, D_CONV + 256 * g:D_CONV + 256 * (g + 1)] = (
            _dot(ol[:, half * g:half * (g + 1)], wuvg_ref[g]).astype(BF16))
    hf_ref[:, 0:D_CONV] = yconv_ref[...].reshape(rows, D_CONV)
    mix = _dot(hf_ref[...], wo_ref[...])
    x1 = x_ref[...].reshape(rows, D_MODEL) + _rms(mix, gpost_ref[...])
    xo_ref[...] = x1.reshape(nseq, tt, D_MODEL)
    hf_ref[...] = _rms(x1, gfpre_ref[...]).astype(BF16)
    acc_ref[...] = jnp.zeros(acc_ref.shape, F32)

    def up_proj(o, w, buf_ref):
        for part in range(2):
            src = slice(D_FF * part + o, D_FF * part + o + w)
            dst = slice(FF_CHUNK * part, FF_CHUNK * part + w)
            buf_ref[:, HALO - 2:HALO, dst] = fso_ref[:, :, src]
            up = _dot(hf_ref[...], wup_ref[:, src]).reshape(nseq, tt, w)
            buf_ref[:, HALO:HALO + tt, dst] = up
            fso_ref[:, :, src] = up[:, tt - 2:tt, :]

    def gate_down(o, w, buf_ref):
        u = []
        for part in range(2):
            src = slice(D_FF * part + o, D_FF * part + o + w)
            dst = slice(FF_CHUNK * part, FF_CHUNK * part + w)
            wc = wfc_ref[:, src]
            u.append((buf_ref[:, HALO - 2:HALO - 2 + tt, dst] * wc[0] + buf_ref[:, HALO - 1:HALO - 1 + tt, dst] * wc[1]
                      + buf_ref[:, HALO:HALO + tt, dst] * wc[2] + bfc_ref[:, src]).reshape(rows, w))
        a, b = u
        gated = (a * (1.0 / (1.0 + jnp.exp(-a))) * b).astype(BF16)
        acc_ref[...] += _dot(gated, wdown_ref[o:o + w, :])

    bufs = (ubuf_ref, vbuf_ref)
    up_proj(*FF_SPLITS[0], bufs[0])
    for c, (o, w) in enumerate(FF_SPLITS):
        if c + 1 < len(FF_SPLITS):
            up_proj(*FF_SPLITS[c + 1], bufs[(c + 1) % 2])
        gate_down(o, w, bufs[c % 2])
    xo_ref[...] = xo_ref[...] + _rms(acc_ref[...], gfpost_ref[...]).reshape(nseq, tt, D_MODEL)


def _post(x, yconv, olat, ffn_state, lw, *, nseq, tt):
    B, T, _ = x.shape
    nb, nt = B // nseq, T // tt
    rows = nseq * tt
    state_block = (nseq, CONV_W - 1, 2 * D_FF)
    in_specs = [
        pl.BlockSpec((nseq, tt, D_MODEL), lambda b, t: (b, t, 0)),
        pl.BlockSpec((nseq, tt, D_CONV), lambda b, t: (b, t, 0)),
        pl.BlockSpec((nseq, tt, N_HEADS * KV_RANK), lambda b, t: (b, t, 0)),
        pl.BlockSpec(state_block, lambda b, t: (b, 0, 0), pipeline_mode=pl.Buffered(1)),
        _const_spec((2, 4 * KV_RANK, 4 * V_HEAD)),
        _const_spec((D_MODEL, D_MODEL)),
        _const_spec((1, D_MODEL)),
        _const_spec((1, D_MODEL)),
        _const_spec((D_MODEL, 2 * D_FF)),
        _const_spec((CONV_W, 2 * D_FF)),
        _const_spec((1, 2 * D_FF)),
        _const_spec((D_FF, D_MODEL)),
        _const_spec((1, D_MODEL)),
    ]
    return pl.pallas_call(
        functools.partial(_post_kernel, nseq=nseq, tt=tt),
        grid=(nb, nt),
        in_specs=in_specs,
        out_specs=(
            pl.BlockSpec((nseq, tt, D_MODEL), lambda b, t: (b, t, 0)),
            pl.BlockSpec(state_block, lambda b, t: (b, 0, 0)),
        ),
        out_shape=(
            jax.ShapeDtypeStruct((B, T, D_MODEL), F32),
            jax.ShapeDtypeStruct((B, CONV_W - 1, 2 * D_FF), F32),
        ),
        scratch_shapes=[
            pltpu.VMEM((nseq, tt + HALO, 2 * FF_CHUNK), F32),
            pltpu.VMEM((nseq, tt + HALO, 2 * FF_CHUNK), F32),
            pltpu.VMEM((rows, D_MODEL), F32),
            pltpu.VMEM((rows, D_MODEL), BF16),
        ],
        compiler_params=pltpu.CompilerParams(
            dimension_semantics=("arbitrary", "arbitrary"), vmem_limit_bytes=VMEM_LIMIT_BYTES),
        name="post_ffn",
    )(x, yconv, olat, ffn_state, lw["w_uvg"], lw["w_o"], lw["g_mix_post"], lw["g_ffn_pre"], lw["w_up"],
      lw["w_ffn_conv"], lw["b_ffn_conv"], lw["w_down"], lw["g_ffn_post"])


def _block_diag(blocks):
    n = len(blocks)
    rows = []
    for i, blk in enumerate(blocks):
        rows.append(jnp.concatenate(
            [blk if j == i else jnp.zeros((blk.shape[0], blocks[j].shape[1]), blk.dtype) for j in range(n)],
            axis=1))
    return jnp.concatenate(rows, axis=0)


def _layer_weights(l, w_in, w_conv, g_qa, w_uq, g_kva, w_uk, w_uv, w_o, g_mix_pre, g_mix_post,
                   w_up, w_ffn_conv, b_ffn_conv, w_down, g_ffn_pre, g_ffn_post):
    d_in = w_in.shape[2]
    uq = w_uq[l].reshape(Q_RANK, N_HEADS, QK_NOPE + QK_ROPE)
    uq = jnp.concatenate([uq[:, :, :QK_NOPE].reshape(Q_RANK, N_HEADS * QK_NOPE),
                          uq[:, :, QK_NOPE:].reshape(Q_RANK, N_HEADS * QK_ROPE)], axis=1)
    uk_t = jnp.transpose(w_uk[l], (1, 2, 0))
    uv = jnp.transpose(w_uv[l], (1, 0, 2))
    return {
        "w1": jnp.pad(w_in[l], ((0, 0), (0, D_IN_PAD - d_in))).astype(BF16),
        "w_conv": w_conv[l],
        "g_mix_pre": g_mix_pre[l][None],
        "g_qa": g_qa[l][None],
        "w_uq": uq.astype(BF16),
        "g_kva": g_kva[l][None],
        "w_ukp": jnp.stack([_block_diag([uk_t[2 * p], uk_t[2 * p + 1]])
                            for p in range(N_HEADS // 2)]).astype(BF16),
        "w_uvg": jnp.stack([_block_diag([uv[4 * g + j] for j in range(4)]) for g in range(2)]).astype(BF16),
        "w_o": w_o[l].astype(BF16),
        "g_mix_post": g_mix_post[l][None],
        "g_ffn_pre": g_ffn_pre[l][None],
        "w_up": w_up[l].astype(BF16),
        "w_ffn_conv": w_ffn_conv[l],
        "b_ffn_conv": b_ffn_conv[l][None],
        "w_down": w_down[l].astype(BF16),
        "g_ffn_post": g_ffn_post[l][None],
    }


def _rope_tables(pos, reps):
    inv = 1.0 / (ROPE_BASE ** (jnp.arange(0, QK_ROPE, 2, dtype=F32) / QK_ROPE))
    ang = pos.astype(F32)[:, None] * inv[None, :]
    cos, sin = jnp.cos(ang), jnp.sin(ang)
    zero = jnp.zeros_like(sin)
    group = 128 // QK_ROPE
    rc = jnp.tile(jnp.concatenate([cos, cos], axis=1), (reps, group))
    rs1 = jnp.tile(jnp.concatenate([-sin, zero], axis=1), (reps, group))
    rs2 = jnp.tile(jnp.concatenate([zero, sin], axis=1), (reps, group))
    return rc, rs1, rs2


def _pick_tile(T, target):
    tt = min(T, target)
    assert T % tt == 0 and tt % 16 == 0
    return tt


def _trunk(x_prompt, x_sample, cache_ckv, cache_kpe, state_conv, state_ffn, weights):
    depth = cache_ckv.shape[0]
    Bp, Tp, _ = x_prompt.shape
    Bs, Ts, _ = x_sample.shape
    past = cache_ckv.shape[2]
    assert Ts >= CONV_W - 1 and Ts % 16 == 0

    tt = _pick_tile(Tp, 512)
    tq = _pick_tile(Tp, 256)
    tk = _pick_tile(Tp, 512)
    ns = math.gcd(Bs, SAMPLE_SEQS_PER_STEP)
    p_ropes = _rope_tables(jnp.arange(Tp), 1)
    s_ropes = _rope_tables(past + jnp.arange(Ts), ns)
    p_conv0 = jnp.zeros((Bp, CONV_W - 1, D_CONV), F32)
    p_ffn0 = jnp.zeros((Bp, CONV_W - 1, 2 * D_FF), F32)

    cache_kpe_t = jnp.swapaxes(cache_kpe, 2, 3)
    xp, xs = x_prompt, x_sample
    outs = [[] for _ in range(8)]
    for l in range(depth):
        lw = _layer_weights(l, *weights)
        yc_p, q_p, kcat_p, ckv_p, kpe_p, conv_p, vt_p = _mix_in(
            xp, p_conv0, lw, p_ropes, nseq=1, tt=tt, vt_tile=tk, qt_tile=tq)
        yc_s, q_s, kcat_s, ckv_s, kpe_s, conv_s = _mix_in(xs, state_conv[l], lw, s_ropes, nseq=ns, tt=Ts)
        ol_p = _attend_prompt(q_p, kcat_p, vt_p)
        ol_s = _attend_sample(q_s, cache_ckv, cache_kpe_t, kcat_s, l)
        xp, ffn_p = _post(xp, yc_p, ol_p, p_ffn0, lw, nseq=1, tt=tt)
        xs, ffn_s = _post(xs, yc_s, ol_s, state_ffn[l], lw, nseq=ns, tt=Ts)
        for acc, v in zip(outs, (ckv_p, kpe_p, conv_p, ffn_p, ckv_s, kpe_s, conv_s, ffn_s)):
            acc.append(v)
    return (xp, xs) + tuple(jnp.stack(v) for v in outs)


def kernel(x_prompt, x_sample, cache_ckv, cache_kpe, state_conv, state_ffn, w_in, w_conv, g_qa, w_uq, g_kva, w_uk, w_uv, w_o, g_mix_pre, g_mix_post, w_up, w_ffn_conv, b_ffn_conv, w_down, g_ffn_pre, g_ffn_post):
    weights = (w_in, w_conv, g_qa, w_uq, g_kva, w_uk, w_uv, w_o, g_mix_pre, g_mix_post,
               w_up, w_ffn_conv, b_ffn_conv, w_down, g_ffn_pre, g_ffn_post)
    return _trunk(x_prompt, x_sample, cache_ckv, cache_kpe, state_conv, state_ffn, weights)
```

```python
import functools
import math

import jax
import jax.numpy as jnp
from jax import lax
from jax.experimental import pallas as pl
from jax.experimental.pallas import tpu as pltpu

D_MODEL = 1024
CHUNK = 64
CONV_W = 3
D_CONV = D_MODEL // 2
N_HEADS = 8
QK_NOPE = 64
QK_ROPE = 32
V_HEAD = 64
KV_RANK = 128
Q_RANK = 256
D_FF = 2816
ROPE_BASE = 10000.0
EPS = 1e-6
ATTN_SCALE = 1.0 / math.sqrt(QK_NOPE + QK_ROPE)
Q_SCALE = ATTN_SCALE * math.log2(math.e)

LANES = 128
SUBLANES = 8
BF16_SUBLANES = 16
MXU_TILE = 256
VMEM_LIMIT_BYTES = 56 * 1024 * 1024

D_QK = KV_RANK + QK_ROPE
D_IN_PAD = 2048
COL_QA = 3 * D_CONV
COL_KV = COL_QA + Q_RANK
FF_SPLITS = ((0, 1280), (1280, 1536))
FF_CHUNK = max(w for _, w in FF_SPLITS)
assert sum(w for _, w in FF_SPLITS) == D_FF and all(o % MXU_TILE == 0 for o, _ in FF_SPLITS)
ATTN_UNROLL = 4
ATTN_COL_GROUPS = 1
VT_ROWS = KV_RANK + BF16_SUBLANES
HALO = SUBLANES
SAMPLE_SEQS_PER_STEP = 16

BF16 = jnp.bfloat16
F32 = jnp.float32


def _rms(x, g):
    return x * lax.rsqrt(jnp.mean(x * x, axis=-1, keepdims=True) + EPS) * g


def _dot(a, b):
    return jnp.dot(a, b, preferred_element_type=F32)


def _dot_nt(a, b):
    return lax.dot_general(a, b, (((1,), (1,)), ((), ())), preferred_element_type=F32)


def _chunk_of(pos):
    return lax.shift_right_logical(pos, CHUNK.bit_length() - 1)


def _rope(x, rc, rs1, rs2):
    return x * rc + pltpu.roll(x, LANES - QK_ROPE // 2, 1) * rs1 + pltpu.roll(x, QK_ROPE // 2, 1) * rs2


def _stage_halo(buf_ref, cols, prev):
    n_prev = prev.shape[1]
    buf_ref[:, 0:HALO - n_prev, cols] = jnp.zeros((prev.shape[0], HALO - n_prev, prev.shape[2]), prev.dtype)
    buf_ref[:, HALO - n_prev:HALO, cols] = prev


def _causal_conv(buf_ref, cur, prev, w):
    tt = cur.shape[1]
    _stage_halo(buf_ref, slice(None), prev)
    buf_ref[:, HALO:HALO + tt, :] = cur
    ext = buf_ref[...]
    s1 = pltpu.roll(ext, 1, 1)[:, HALO:HALO + tt, :]
    s2 = pltpu.roll(ext, 2, 1)[:, HALO:HALO + tt, :]
    return s2 * w[0] + s1 * w[1] + cur * w[2]


def _mix_in_kernel(x_ref, cs_ref, w1_ref, wconv_ref, gpre_ref, gqa_ref, wuq_ref, gkva_ref, wukp_ref,
                   rc_ref, rs1_ref, rs2_ref,
                   yconv_ref, q_ref, kcat_ref, ckv_ref, kpe_ref, cso_ref, *rest, nseq, tt, vt_tile, qt_tile):
    cbuf_ref = rest[-1]
    rows = nseq * tt

    @pl.when(pl.program_id(1) == 0)
    def _():
        cso_ref[...] = cs_ref[...]

    x = x_ref[...].reshape(rows, D_MODEL)
    h = _rms(x, gpre_ref[...]).astype(BF16)

    qa = _dot(h, w1_ref[:, COL_QA:COL_KV])
    kv = _dot(h, w1_ref[:, COL_KV:D_IN_PAD])
    xv = _dot(h, w1_ref[:, 0:D_CONV])
    gc = _dot(h, w1_ref[:, 2 * D_CONV:3 * D_CONV])
    gb = _dot(h, w1_ref[:, D_CONV:2 * D_CONV])
    qn = _rms(qa, gqa_ref[...]).astype(BF16)
    q = _dot(qn, wuq_ref[...])
    pair = 2 * QK_NOPE
    qls = [_dot(q[:, pair * p:pair * (p + 1)].astype(BF16), wukp_ref[p]) * Q_SCALE for p in range(N_HEADS // 2)]

    c = (gc * xv).reshape(nseq, tt, D_CONV)
    conv = _causal_conv(cbuf_ref, c, cso_ref[...], wconv_ref[...])
    yconv_ref[...] = (gb.reshape(nseq, tt, D_CONV) * conv).astype(BF16)
    cso_ref[...] = c[:, tt - 2:tt, :]

    rc = rc_ref[...]
    rs1 = rs1_ref[...]
    rs2 = rs2_ref[...]

    n_nope = N_HEADS * QK_NOPE
    heads_per_tile = LANES // QK_ROPE
    for p in range(N_HEADS // 2):
        ql = qls[p]
        for j in range(2):
            hd = 2 * p + j
            qh = ql[:, KV_RANK * j:KV_RANK * (j + 1)]
            if qt_tile:
                qh_t = qh.T.astype(BF16)
                for i in range(tt // qt_tile):
                    q_ref[0, i, 0:KV_RANK, qt_tile * hd:qt_tile * (hd + 1)] = (
                        qh_t[:, qt_tile * i:qt_tile * (i + 1)])
            else:
                q_ref[:, hd, :, 0:KV_RANK] = qh.reshape(nseq, tt, KV_RANK).astype(BF16)
    for half in range(N_HEADS // heads_per_tile):
        qp = _rope(q[:, n_nope + LANES * half:n_nope + LANES * (half + 1)], rc, rs1, rs2) * Q_SCALE
        qp_t = qp.T.astype(BF16) if qt_tile else None
        for j in range(heads_per_tile):
            hd = heads_per_tile * half + j
            if qt_tile:
                for i in range(tt // qt_tile):
                    q_ref[0, i, KV_RANK:D_QK, qt_tile * hd:qt_tile * (hd + 1)] = (
                        qp_t[QK_ROPE * j:QK_ROPE * (j + 1), qt_tile * i:qt_tile * (i + 1)])
            else:
                q_ref[:, hd, :, KV_RANK:D_QK] = (
                    qp[:, QK_ROPE * j:QK_ROPE * (j + 1)].reshape(nseq, tt, QK_ROPE).astype(BF16))

    ckv = _rms(kv[:, 0:KV_RANK], gkva_ref[...])
    kpe = _rope(kv[:, KV_RANK:2 * KV_RANK], rc, rs1, rs2)[:, 0:QK_ROPE]
    ckv_ref[...] = ckv.reshape(nseq, tt, KV_RANK)
    kpe_ref[...] = kpe.reshape(nseq, tt, QK_ROPE)
    kcat_ref[:, :, 0:KV_RANK] = ckv.reshape(nseq, tt, KV_RANK).astype(BF16)
    kcat_ref[:, :, KV_RANK:D_QK] = kpe.reshape(nseq, tt, QK_ROPE).astype(BF16)
    if vt_tile:
        vt_ref = rest[0]
        for i in range(tt // vt_tile):
            vt_ref[0, i, 0:KV_RANK, :] = ckv[vt_tile * i:vt_tile * (i + 1), :].T.astype(BF16)
            vt_ref[0, i, KV_RANK:VT_ROWS, :] = jnp.ones((VT_ROWS - KV_RANK, vt_tile), BF16)


def _const_spec(shape):
    nd = len(shape)
    return pl.BlockSpec(shape, lambda *_: (0,) * nd, pipeline_mode=pl.Buffered(1))


def _mix_in(x, conv_state, lw, ropes, *, nseq, tt, vt_tile=0, qt_tile=0):
    B, T, _ = x.shape
    assert not vt_tile or (nseq == 1 and tt % vt_tile == 0)
    assert not qt_tile or (nseq == 1 and tt % qt_tile == 0)
    if qt_tile:
        q_shape = jax.ShapeDtypeStruct((B, T // qt_tile, D_QK, N_HEADS * qt_tile), BF16)
        q_spec = pl.BlockSpec((1, tt // qt_tile, D_QK, N_HEADS * qt_tile), lambda b, t: (b, t, 0, 0))
    else:
        q_shape = jax.ShapeDtypeStruct((B, N_HEADS, T, D_QK), BF16)
        q_spec = pl.BlockSpec((nseq, N_HEADS, tt, D_QK), lambda b, t: (b, 0, t, 0))
    nb, nt = B // nseq, T // tt
    rows = nseq * tt
    rope_spec = pl.BlockSpec((rows, LANES), lambda b, t: (t, 0))
    in_specs = [
        pl.BlockSpec((nseq, tt, D_MODEL), lambda b, t: (b, t, 0)),
        pl.BlockSpec((nseq, CONV_W - 1, D_CONV), lambda b, t: (b, 0, 0)),
        _const_spec((D_MODEL, D_IN_PAD)),
        _const_spec((CONV_W, D_CONV)),
        _const_spec((1, D_MODEL)),
        _const_spec((1, Q_RANK)),
        _const_spec((Q_RANK, N_HEADS * (QK_NOPE + QK_ROPE))),
        _const_spec((1, KV_RANK)),
        _const_spec((N_HEADS // 2, 2 * QK_NOPE, 2 * KV_RANK)),
        rope_spec, rope_spec, rope_spec,
    ]
    out_shape = [
        jax.ShapeDtypeStruct((B, T, D_CONV), BF16),
        q_shape,
        jax.ShapeDtypeStruct((B, T, D_QK), BF16),
        jax.ShapeDtypeStruct((B, T, KV_RANK), F32),
        jax.ShapeDtypeStruct((B, T, QK_ROPE), F32),
        jax.ShapeDtypeStruct((B, CONV_W - 1, D_CONV), F32),
    ]
    out_specs = [
        pl.BlockSpec((nseq, tt, D_CONV), lambda b, t: (b, t, 0)),
        q_spec,
        pl.BlockSpec((nseq, tt, D_QK), lambda b, t: (b, t, 0)),
        pl.BlockSpec((nseq, tt, KV_RANK), lambda b, t: (b, t, 0)),
        pl.BlockSpec((nseq, tt, QK_ROPE), lambda b, t: (b, t, 0)),
        pl.BlockSpec((nseq, CONV_W - 1, D_CONV), lambda b, t: (b, 0, 0)),
    ]
    if vt_tile:
        out_shape.append(jax.ShapeDtypeStruct((B, T // vt_tile, VT_ROWS, vt_tile), BF16))
        out_specs.append(pl.BlockSpec((1, tt // vt_tile, VT_ROWS, vt_tile), lambda b, t: (b, t, 0, 0)))
    return pl.pallas_call(
        functools.partial(_mix_in_kernel, nseq=nseq, tt=tt, vt_tile=vt_tile, qt_tile=qt_tile),
        grid=(nb, nt),
        in_specs=in_specs,
        out_specs=out_specs,
        out_shape=out_shape,
        scratch_shapes=[pltpu.VMEM((nseq, tt + HALO, D_CONV), F32)],
        compiler_params=pltpu.CompilerParams(
            dimension_semantics=("arbitrary", "arbitrary"), vmem_limit_bytes=VMEM_LIMIT_BYTES),
        name="mix_in",
    )(x, conv_state, lw["w1"], lw["w_conv"], lw["g_mix_pre"], lw["g_qa"], lw["w_uq"], lw["g_kva"],
      lw["w_ukp"], *ropes)


def _attend_prompt_kernel(q_ref, qn_ref, k_ref, vt_ref, o_ref, m_ref, acc_ref,
                          s0_ref, sa_ref, sb_ref, s0_max_ref, sa_max_ref, sb_max_ref, *, tq, tk):
    cols = N_HEADS * tq
    gw = cols // ATTN_COL_GROUPS
    qi = pl.program_id(1)
    q0 = qi * tq
    n_last = q0 // tk

    m_ref[...] = jnp.full(m_ref.shape, -jnp.inf, F32)
    acc_ref[...] = jnp.zeros(acc_ref.shape, F32)

    def scores(j, buf, query_ref=q_ref):
        s_ref, smax_ref = buf
        k0 = pl.multiple_of(j * tk, tk)
        s = _dot(k_ref[0, pl.ds(k0, tk), :], query_ref[0, 0])
        s_ref[...] = s
        smax_ref[...] = jnp.max(s, axis=0, keepdims=True)

    def consume(j, buf, masked):
        s_ref, smax_ref = buf
        vt = vt_ref[0, j]
        for g in range(ATTN_COL_GROUPS):
            sl = slice(gw * g, gw * (g + 1))
            s = s_ref[:, sl]
            if masked:
                k_pos = j * tk + lax.broadcasted_iota(jnp.int32, (tk, gw), 0)
                q_pos = q0 + (lax.broadcasted_iota(jnp.int32, (tk, gw), 1) & (tq - 1))
                s = jnp.where(_chunk_of(k_pos) <= _chunk_of(q_pos), s, -jnp.inf)
                s_max = jnp.max(s, axis=0, keepdims=True)
            else:
                s_max = smax_ref[:, sl]
            m_prev = m_ref[:, sl]
            m_new = jnp.maximum(m_prev, s_max)
            p = jnp.exp2(s - m_new).astype(BF16)
            acc_ref[:, sl] = jnp.exp2(m_prev - m_new) * acc_ref[:, sl] + _dot(vt, p)
            m_ref[:, sl] = m_new

    first = (s0_ref, s0_max_ref)
    bufs = ((sa_ref, sa_max_ref), (sb_ref, sb_max_ref))

    def score_next_query_tile():
        scores(0, first, qn_ref)

    def run(start, count):
        for t in range(count):
            scores(start + t + 1, bufs[(t + 1) % 2])
            consume(start + t, bufs[t % 2], False)

    @pl.when(qi == 0)
    def _():
        scores(0, first)

    @pl.when(n_last == 0)
    def _():
        consume(0, first, True)
        score_next_query_tile()

    @pl.when(n_last > 0)
    def _():
        scores(1, bufs[0])
        consume(0, first, False)

    later = jnp.maximum(n_last - 1, 0)

    def unrolled_body(i, carry):
        run(1 + ATTN_UNROLL * i, ATTN_UNROLL)
        return carry

    lax.fori_loop(0, later // ATTN_UNROLL, unrolled_body, 0)
    done = (later // ATTN_UNROLL) * ATTN_UNROLL
    for rest in range(ATTN_UNROLL):
        @pl.when((n_last > 0) & (later - done == rest))
        def _():
            run(1 + done, rest)
            score_next_query_tile()
            consume(n_last, bufs[rest % 2], True)

    o = acc_ref[0:KV_RANK, :] / acc_ref[KV_RANK:KV_RANK + 1, :]
    for h in range(N_HEADS):
        o_ref[0, :, KV_RANK * h:KV_RANK * (h + 1)] = o[:, tq * h:tq * (h + 1)].T.astype(BF16)


def _attend_prompt(q_t, kcat, vt):
    B, T, _ = kcat.shape
    tk = vt.shape[3]
    cols = q_t.shape[3]
    tq = cols // N_HEADS
    nq = T // tq
    assert tq % CHUNK == 0 and tq & (tq - 1) == 0 and T % tk == 0 and tk % tq == 0
    score_buf = pltpu.VMEM((tk, cols), F32)
    max_buf = pltpu.VMEM((1, cols), F32)
    return pl.pallas_call(
        functools.partial(_attend_prompt_kernel, tq=tq, tk=tk),
        grid=(B, nq),
        in_specs=[
            pl.BlockSpec((1, 1, D_QK, cols), lambda b, i: (b, i, 0, 0)),
            pl.BlockSpec((1, 1, D_QK, cols), lambda b, i: (b, jnp.minimum(i + 1, nq - 1), 0, 0)),
            pl.BlockSpec((1, T, D_QK), lambda b, i: (b, 0, 0)),
            pl.BlockSpec((1, T // tk, VT_ROWS, tk), lambda b, i: (b, 0, 0, 0)),
        ],
        out_specs=pl.BlockSpec((1, tq, N_HEADS * KV_RANK), lambda b, i: (b, i, 0)),
        out_shape=jax.ShapeDtypeStruct((B, T, N_HEADS * KV_RANK), BF16),
        scratch_shapes=[
            pltpu.VMEM((1, cols), F32),
            pltpu.VMEM((VT_ROWS, cols), F32),
            score_buf, score_buf, score_buf,
            max_buf, max_buf, max_buf,
        ],
        compiler_params=pltpu.CompilerParams(
            dimension_semantics=("arbitrary", "arbitrary"), vmem_limit_bytes=VMEM_LIMIT_BYTES),
        name="attend_prompt",
    )(q_t, q_t, kcat, vt)


def _attend_sample_kernel(q_ref, ckv_ref, kpe_ref, kn_ref, o_ref, *, ts, past):
    rows = N_HEADS * ts
    q = q_ref[0].reshape(rows, D_QK)
    ck = ckv_ref[0, 0].astype(BF16)
    kp_t = kpe_ref[0, 0].astype(BF16)
    kn = kn_ref[0]
    s_old = _dot_nt(q[:, 0:KV_RANK], ck) + _dot(q[:, KV_RANK:D_QK], kp_t)
    s_new = _dot_nt(q, kn)
    q_pos = past + (lax.broadcasted_iota(jnp.int32, (rows, ts), 0) & (ts - 1))
    k_pos = past + lax.broadcasted_iota(jnp.int32, (rows, ts), 1)
    s_new = jnp.where(_chunk_of(k_pos) <= _chunk_of(q_pos), s_new, -jnp.inf)
    m = jnp.maximum(jnp.max(s_old, axis=1, keepdims=True), jnp.max(s_new, axis=1, keepdims=True))
    p_old = jnp.exp2(s_old - m)
    p_new = jnp.exp2(s_new - m)
    l = jnp.sum(p_old, axis=1, keepdims=True) + jnp.sum(p_new, axis=1, keepdims=True)
    o = (_dot(p_old.astype(BF16), ck) + _dot(p_new.astype(BF16), kn[:, 0:KV_RANK])) / l
    for h in range(N_HEADS):
        o_ref[0, :, KV_RANK * h:KV_RANK * (h + 1)] = o[ts * h:ts * (h + 1), :].astype(BF16)


def _attend_sample(q, cache_ckv, cache_kpe_t, kcat, layer):
    B, _, ts, _ = q.shape
    past = cache_ckv.shape[2]
    return pl.pallas_call(
        functools.partial(_attend_sample_kernel, ts=ts, past=past),
        grid=(B,),
        in_specs=[
            pl.BlockSpec((1, N_HEADS, ts, D_QK), lambda b: (b, 0, 0, 0)),
            pl.BlockSpec((1, 1, past, KV_RANK), lambda b: (layer, b, 0, 0)),
            pl.BlockSpec((1, 1, QK_ROPE, past), lambda b: (layer, b, 0, 0)),
            pl.BlockSpec((1, ts, D_QK), lambda b: (b, 0, 0)),
        ],
        out_specs=pl.BlockSpec((1, ts, N_HEADS * KV_RANK), lambda b: (b, 0, 0)),
        out_shape=jax.ShapeDtypeStruct((B, ts, N_HEADS * KV_RANK), BF16),
        compiler_params=pltpu.CompilerParams(
            dimension_semantics=("arbitrary",), vmem_limit_bytes=VMEM_LIMIT_BYTES),
        name="attend_sample",
    )(q, cache_ckv, cache_kpe_t, kcat)


def _post_kernel(x_ref, yconv_ref, olat_ref, fs_ref, wuvg_ref, wo_ref, gpost_ref, gfpre_ref, wup_ref,
                 wfc_ref, bfc_ref, wdown_ref, gfpost_ref,
                 xo_ref, fso_ref,
                 ubuf_ref, vbuf_ref, acc_ref, hf_ref, *, nseq, tt):
    rows = nseq * tt

    @pl.when(pl.program_id(1) == 0)
    def _():
        fso_ref[...] = fs_ref[...]

    half = D_MODEL // 2
    ol = olat_ref[...].reshape(rows, N_HEADS * KV_RANK)
    for g in range(2):
        hf_ref[:, D_CONV + 4 * V_HEAD * g:D_CONV + 4 * V_HEAD * (g + 1)] = (
            _dot(ol[:, half * g:half * (g + 1)], wuvg_ref[g]).astype(BF16))
    hf_ref[:, 0:D_CONV] = yconv_ref[...].reshape(rows, D_CONV)
    mix = _dot(hf_ref[...], wo_ref[...])
    x1 = x_ref[...].reshape(rows, D_MODEL) + _rms(mix, gpost_ref[...])
    xo_ref[...] = x1.reshape(nseq, tt, D_MODEL)
    hf_ref[...] = _rms(x1, gfpre_ref[...]).astype(BF16)
    acc_ref[...] = jnp.zeros(acc_ref.shape, F32)

    def up_proj(o, w, buf_ref):
        for part in range(2):
            src = slice(D_FF * part + o, D_FF * part + o + w)
            dst = slice(FF_CHUNK * part, FF_CHUNK * part + w)
            _stage_halo(buf_ref, dst, fso_ref[:, :, src])
            up = _dot(hf_ref[...], wup_ref[:, src]).reshape(nseq, tt, w)
            buf_ref[:, HALO:HALO + tt, dst] = up
            fso_ref[:, :, src] = up[:, tt - 2:tt, :]

    def gate_down(o, w, buf_ref):
        u = []
        for part in range(2):
            src = slice(D_FF * part + o, D_FF * part + o + w)
            dst = slice(FF_CHUNK * part, FF_CHUNK * part + w)
            wc = wfc_ref[:, src]
            ext = buf_ref[:, :, dst]
            s1 = pltpu.roll(ext, 1, 1)[:, HALO:HALO + tt, :]
            s2 = pltpu.roll(ext, 2, 1)[:, HALO:HALO + tt, :]
            u.append((s2 * wc[0] + s1 * wc[1] + ext[:, HALO:HALO + tt, :] * wc[2] + bfc_ref[:, src]).reshape(rows, w))
        a, b = u
        gated = (a * (1.0 / (1.0 + jnp.exp(-a))) * b).astype(BF16)
        acc_ref[...] += _dot(gated, wdown_ref[o:o + w, :])

    bufs = (ubuf_ref, vbuf_ref)
    up_proj(*FF_SPLITS[0], bufs[0])
    for c, (o, w) in enumerate(FF_SPLITS):
        if c + 1 < len(FF_SPLITS):
            up_proj(*FF_SPLITS[c + 1], bufs[(c + 1) % 2])
        gate_down(o, w, bufs[c % 2])
    xo_ref[...] = xo_ref[...] + _rms(acc_ref[...], gfpost_ref[...]).reshape(nseq, tt, D_MODEL)


def _post(x, yconv, olat, ffn_state, lw, *, nseq, tt):
    B, T, _ = x.shape
    nb, nt = B // nseq, T // tt
    rows = nseq * tt
    state_block = (nseq, CONV_W - 1, 2 * D_FF)
    in_specs = [
        pl.BlockSpec((nseq, tt, D_MODEL), lambda b, t: (b, t, 0)),
        pl.BlockSpec((nseq, tt, D_CONV), lambda b, t: (b, t, 0)),
        pl.BlockSpec((nseq, tt, N_HEADS * KV_RANK), lambda b, t: (b, t, 0)),
        pl.BlockSpec(state_block, lambda b, t: (b, 0, 0), pipeline_mode=pl.Buffered(1)),
        _const_spec((2, 4 * KV_RANK, 4 * V_HEAD)),
        _const_spec((D_MODEL, D_MODEL)),
        _const_spec((1, D_MODEL)),
        _const_spec((1, D_MODEL)),
        _const_spec((D_MODEL, 2 * D_FF)),
        _const_spec((CONV_W, 2 * D_FF)),
        _const_spec((1, 2 * D_FF)),
        _const_spec((D_FF, D_MODEL)),
        _const_spec((1, D_MODEL)),
    ]
    return pl.pallas_call(
        functools.partial(_post_kernel, nseq=nseq, tt=tt),
        grid=(nb, nt),
        in_specs=in_specs,
        out_specs=(
            pl.BlockSpec((nseq, tt, D_MODEL), lambda b, t: (b, t, 0)),
            pl.BlockSpec(state_block, lambda b, t: (b, 0, 0)),
        ),
        out_shape=(
            jax.ShapeDtypeStruct((B, T, D_MODEL), F32),
            jax.ShapeDtypeStruct((B, CONV_W - 1, 2 * D_FF), F32),
        ),
        scratch_shapes=[
            pltpu.VMEM((nseq, tt + HALO, 2 * FF_CHUNK), F32),
            pltpu.VMEM((nseq, tt + HALO, 2 * FF_CHUNK), F32),
            pltpu.VMEM((rows, D_MODEL), F32),
            pltpu.VMEM((rows, D_MODEL), BF16),
        ],
        compiler_params=pltpu.CompilerParams(
            dimension_semantics=("arbitrary", "arbitrary"), vmem_limit_bytes=VMEM_LIMIT_BYTES),
        name="post_ffn",
    )(x, yconv, olat, ffn_state, lw["w_uvg"], lw["w_o"], lw["g_mix_post"], lw["g_ffn_pre"], lw["w_up"],
      lw["w_ffn_conv"], lw["b_ffn_conv"], lw["w_down"], lw["g_ffn_post"])


def _block_diag(blocks):
    n = len(blocks)
    rows = []
    for i, blk in enumerate(blocks):
        rows.append(jnp.concatenate(
            [blk if j == i else jnp.zeros((blk.shape[0], blocks[j].shape[1]), blk.dtype) for j in range(n)],
            axis=1))
    return jnp.concatenate(rows, axis=0)


def _layer_weights(l, w_in, w_conv, g_qa, w_uq, g_kva, w_uk, w_uv, w_o, g_mix_pre, g_mix_post,
                   w_up, w_ffn_conv, b_ffn_conv, w_down, g_ffn_pre, g_ffn_post):
    d_in = w_in.shape[2]
    uq = w_uq[l].reshape(Q_RANK, N_HEADS, QK_NOPE + QK_ROPE)
    uq = jnp.concatenate([uq[:, :, :QK_NOPE].reshape(Q_RANK, N_HEADS * QK_NOPE),
                          uq[:, :, QK_NOPE:].reshape(Q_RANK, N_HEADS * QK_ROPE)], axis=1)
    uk_t = jnp.transpose(w_uk[l], (1, 2, 0))
    uv = jnp.transpose(w_uv[l], (1, 0, 2))
    return {
        "w1": jnp.pad(w_in[l], ((0, 0), (0, D_IN_PAD - d_in))).astype(BF16),
        "w_conv": w_conv[l],
        "g_mix_pre": g_mix_pre[l][None],
        "g_qa": g_qa[l][None],
        "w_uq": uq.astype(BF16),
        "g_kva": g_kva[l][None],
        "w_ukp": jnp.stack([_block_diag([uk_t[2 * p], uk_t[2 * p + 1]])
                            for p in range(N_HEADS // 2)]).astype(BF16),
        "w_uvg": jnp.stack([_block_diag([uv[4 * g + j] for j in range(4)]) for g in range(2)]).astype(BF16),
        "w_o": w_o[l].astype(BF16),
        "g_mix_post": g_mix_post[l][None],
        "g_ffn_pre": g_ffn_pre[l][None],
        "w_up": w_up[l].astype(BF16),
        "w_ffn_conv": w_ffn_conv[l],
        "b_ffn_conv": b_ffn_conv[l][None],
        "w_down": w_down[l].astype(BF16),
        "g_ffn_post": g_ffn_post[l][None],
    }


def _rope_tables(pos, reps):
    inv = 1.0 / (ROPE_BASE ** (jnp.arange(0, QK_ROPE, 2, dtype=F32) / QK_ROPE))
    ang = pos.astype(F32)[:, None] * inv[None, :]
    cos, sin = jnp.cos(ang), jnp.sin(ang)
    zero = jnp.zeros_like(sin)
    group = LANES // QK_ROPE
    rc = jnp.tile(jnp.concatenate([cos, cos], axis=1), (reps, group))
    rs1 = jnp.tile(jnp.concatenate([-sin, zero], axis=1), (reps, group))
    rs2 = jnp.tile(jnp.concatenate([zero, sin], axis=1), (reps, group))
    return rc, rs1, rs2


def _pick_tile(T, target):
    tt = min(T, target)
    assert T % tt == 0 and tt % BF16_SUBLANES == 0
    return tt


def _trunk(x_prompt, x_sample, cache_ckv, cache_kpe, state_conv, state_ffn, weights):
    depth = cache_ckv.shape[0]
    Bp, Tp, _ = x_prompt.shape
    Bs, Ts, _ = x_sample.shape
    past = cache_ckv.shape[2]
    assert Ts >= CONV_W - 1 and Ts % BF16_SUBLANES == 0

    tt = _pick_tile(Tp, 512)
    tq = _pick_tile(Tp, 256)
    tk = _pick_tile(Tp, 512)
    ns = math.gcd(Bs, SAMPLE_SEQS_PER_STEP)
    p_ropes = _rope_tables(jnp.arange(Tp), 1)
    s_ropes = _rope_tables(past + jnp.arange(Ts), ns)
    p_conv0 = jnp.zeros((Bp, CONV_W - 1, D_CONV), F32)
    p_ffn0 = jnp.zeros((Bp, CONV_W - 1, 2 * D_FF), F32)

    cache_kpe_t = jnp.swapaxes(cache_kpe, 2, 3)
    xp, xs = x_prompt, x_sample
    outs = [[] for _ in range(8)]
    for l in range(depth):
        lw = _layer_weights(l, *weights)
        yc_p, q_p, kcat_p, ckv_p, kpe_p, conv_p, vt_p = _mix_in(
            xp, p_conv0, lw, p_ropes, nseq=1, tt=tt, vt_tile=tk, qt_tile=tq)
        yc_s, q_s, kcat_s, ckv_s, kpe_s, conv_s = _mix_in(xs, state_conv[l], lw, s_ropes, nseq=ns, tt=Ts)
        ol_p = _attend_prompt(q_p, kcat_p, vt_p)
        ol_s = _attend_sample(q_s, cache_ckv, cache_kpe_t, kcat_s, l)
        xp, ffn_p = _post(xp, yc_p, ol_p, p_ffn0, lw, nseq=1, tt=tt)
        xs, ffn_s = _post(xs, yc_s, ol_s, state_ffn[l], lw, nseq=ns, tt=Ts)
        for acc, v in zip(outs, (ckv_p, kpe_p, conv_p, ffn_p, ckv_s, kpe_s, conv_s, ffn_s)):
            acc.append(v)
    return (xp, xs) + tuple(jnp.stack(v) for v in outs)


def kernel(x_prompt, x_sample, cache_ckv, cache_kpe, state_conv, state_ffn, w_in, w_conv, g_qa, w_uq, g_kva, w_uk, w_uv, w_o, g_mix_pre, g_mix_post, w_up, w_ffn_conv, b_ffn_conv, w_down, g_ffn_pre, g_ffn_post):
    weights = (w_in, w_conv, g_qa, w_uq, g_kva, w_uk, w_uv, w_o, g_mix_pre, g_mix_post,
               w_up, w_ffn_conv, b_ffn_conv, w_down, g_ffn_pre, g_ffn_post)
    return _trunk(x_prompt, x_sample, cache_ckv, cache_kpe, state_conv, state_ffn, weights)
```

```python
import functools
import math

import jax
import jax.numpy as jnp
from jax import lax
from jax.experimental import pallas as pl
from jax.experimental.pallas import tpu as pltpu

D_MODEL = 1024
CHUNK = 64
CONV_W = 3
D_CONV = D_MODEL // 2
N_HEADS = 8
QK_NOPE = 64
QK_ROPE = 32
V_HEAD = 64
KV_RANK = 128
Q_RANK = 256
D_FF = 2816
ROPE_BASE = 10000.0
EPS = 1e-6
ATTN_SCALE = 1.0 / math.sqrt(QK_NOPE + QK_ROPE)
Q_SCALE = ATTN_SCALE * math.log2(math.e)

LANES = 128
SUBLANES = 8
BF16_SUBLANES = 16
MXU_TILE = 256
VMEM_LIMIT_BYTES = 56 * 1024 * 1024

D_QK = KV_RANK + QK_ROPE
D_IN_PAD = 2048
COL_QA = 3 * D_CONV
COL_KV = COL_QA + Q_RANK
FF_SPLITS = ((0, 1280), (1280, 1536))
FF_CHUNK = max(w for _, w in FF_SPLITS)
assert sum(w for _, w in FF_SPLITS) == D_FF and all(o % MXU_TILE == 0 for o, _ in FF_SPLITS)
ATTN_UNROLL = 4
ATTN_COL_GROUPS = 2
VT_ROWS = KV_RANK + BF16_SUBLANES
HALO = SUBLANES
SAMPLE_SEQS_PER_STEP = 16

BF16 = jnp.bfloat16
F32 = jnp.float32


def _rms(x, g):
    return x * lax.rsqrt(jnp.mean(x * x, axis=-1, keepdims=True) + EPS) * g


def _dot(a, b):
    return jnp.dot(a, b, preferred_element_type=F32)


def _dot_nt(a, b):
    return lax.dot_general(a, b, (((1,), (1,)), ((), ())), preferred_element_type=F32)


def _chunk_of(pos):
    return lax.shift_right_logical(pos, CHUNK.bit_length() - 1)


def _rope(x, rc, rs1, rs2):
    return x * rc + pltpu.roll(x, LANES - QK_ROPE // 2, 1) * rs1 + pltpu.roll(x, QK_ROPE // 2, 1) * rs2


def _stage_halo(buf_ref, cols, prev):
    n_prev = prev.shape[1]
    buf_ref[:, 0:HALO - n_prev, cols] = jnp.zeros((prev.shape[0], HALO - n_prev, prev.shape[2]), prev.dtype)
    buf_ref[:, HALO - n_prev:HALO, cols] = prev


def _causal_conv(buf_ref, cur, prev, w):
    tt = cur.shape[1]
    _stage_halo(buf_ref, slice(None), prev)
    buf_ref[:, HALO:HALO + tt, :] = cur
    ext = buf_ref[...]
    s1 = pltpu.roll(ext, 1, 1)[:, HALO:HALO + tt, :]
    s2 = pltpu.roll(ext, 2, 1)[:, HALO:HALO + tt, :]
    return s2 * w[0] + s1 * w[1] + cur * w[2]


def _mix_in_kernel(x_ref, cs_ref, w1_ref, wconv_ref, gpre_ref, gqa_ref, wuq_ref, gkva_ref, wukp_ref,
                   rc_ref, rs1_ref, rs2_ref,
                   yconv_ref, q_ref, kcat_ref, ckv_ref, kpe_ref, cso_ref, *rest, nseq, tt, vt_tile, qt_tile):
    cbuf_ref = rest[-1]
    rows = nseq * tt

    @pl.when(pl.program_id(1) == 0)
    def _():
        cso_ref[...] = cs_ref[...]

    x = x_ref[...].reshape(rows, D_MODEL)
    h = _rms(x, gpre_ref[...]).astype(BF16)

    qa = _dot(h, w1_ref[:, COL_QA:COL_KV])
    kv = _dot(h, w1_ref[:, COL_KV:D_IN_PAD])
    xv = _dot(h, w1_ref[:, 0:D_CONV])
    gc = _dot(h, w1_ref[:, 2 * D_CONV:3 * D_CONV])
    gb = _dot(h, w1_ref[:, D_CONV:2 * D_CONV])
    qn = _rms(qa, gqa_ref[...]).astype(BF16)
    q = _dot(qn, wuq_ref[...])
    pair = 2 * QK_NOPE
    qls = [_dot(q[:, pair * p:pair * (p + 1)].astype(BF16), wukp_ref[p]) * Q_SCALE for p in range(N_HEADS // 2)]

    c = (gc * xv).reshape(nseq, tt, D_CONV)
    conv = _causal_conv(cbuf_ref, c, cso_ref[...], wconv_ref[...])
    yconv_ref[...] = (gb.reshape(nseq, tt, D_CONV) * conv).astype(BF16)
    cso_ref[...] = c[:, tt - 2:tt, :]

    rc = rc_ref[...]
    rs1 = rs1_ref[...]
    rs2 = rs2_ref[...]

    n_nope = N_HEADS * QK_NOPE
    heads_per_tile = LANES // QK_ROPE
    for p in range(N_HEADS // 2):
        ql = qls[p]
        for j in range(2):
            hd = 2 * p + j
            qh = ql[:, KV_RANK * j:KV_RANK * (j + 1)]
            if qt_tile:
                qh_t = qh.T.astype(BF16)
                for i in range(tt // qt_tile):
                    q_ref[0, i, 0:KV_RANK, qt_tile * hd:qt_tile * (hd + 1)] = (
                        qh_t[:, qt_tile * i:qt_tile * (i + 1)])
            else:
                q_ref[:, hd, :, 0:KV_RANK] = qh.reshape(nseq, tt, KV_RANK).astype(BF16)
    for half in range(N_HEADS // heads_per_tile):
        qp = _rope(q[:, n_nope + LANES * half:n_nope + LANES * (half + 1)], rc, rs1, rs2) * Q_SCALE
        qp_t = qp.T.astype(BF16) if qt_tile else None
        for j in range(heads_per_tile):
            hd = heads_per_tile * half + j
            if qt_tile:
                for i in range(tt // qt_tile):
                    q_ref[0, i, KV_RANK:D_QK, qt_tile * hd:qt_tile * (hd + 1)] = (
                        qp_t[QK_ROPE * j:QK_ROPE * (j + 1), qt_tile * i:qt_tile * (i + 1)])
            else:
                q_ref[:, hd, :, KV_RANK:D_QK] = (
                    qp[:, QK_ROPE * j:QK_ROPE * (j + 1)].reshape(nseq, tt, QK_ROPE).astype(BF16))

    ckv = _rms(kv[:, 0:KV_RANK], gkva_ref[...])
    kpe = _rope(kv[:, KV_RANK:2 * KV_RANK], rc, rs1, rs2)[:, 0:QK_ROPE]
    ckv_ref[...] = ckv.reshape(nseq, tt, KV_RANK)
    kpe_ref[...] = kpe.reshape(nseq, tt, QK_ROPE)
    kcat_ref[:, :, 0:KV_RANK] = ckv.reshape(nseq, tt, KV_RANK).astype(BF16)
    kcat_ref[:, :, KV_RANK:D_QK] = kpe.reshape(nseq, tt, QK_ROPE).astype(BF16)
    if vt_tile:
        vt_ref = rest[0]
        for i in range(tt // vt_tile):
            vt_ref[0, i, 0:KV_RANK, :] = ckv[vt_tile * i:vt_tile * (i + 1), :].T.astype(BF16)
            vt_ref[0, i, KV_RANK:VT_ROWS, :] = jnp.ones((VT_ROWS - KV_RANK, vt_tile), BF16)


def _const_spec(shape):
    nd = len(shape)
    return pl.BlockSpec(shape, lambda *_: (0,) * nd, pipeline_mode=pl.Buffered(1))


def _mix_in(x, conv_state, lw, ropes, *, nseq, tt, vt_tile=0, qt_tile=0):
    B, T, _ = x.shape
    assert not vt_tile or (nseq == 1 and tt % vt_tile == 0)
    assert not qt_tile or (nseq == 1 and tt % qt_tile == 0)
    if qt_tile:
        q_shape = jax.ShapeDtypeStruct((B, T // qt_tile, D_QK, N_HEADS * qt_tile), BF16)
        q_spec = pl.BlockSpec((1, tt // qt_tile, D_QK, N_HEADS * qt_tile), lambda b, t: (b, t, 0, 0))
    else:
        q_shape = jax.ShapeDtypeStruct((B, N_HEADS, T, D_QK), BF16)
        q_spec = pl.BlockSpec((nseq, N_HEADS, tt, D_QK), lambda b, t: (b, 0, t, 0))
    nb, nt = B // nseq, T // tt
    rows = nseq * tt
    rope_spec = pl.BlockSpec((rows, LANES), lambda b, t: (t, 0))
    in_specs = [
        pl.BlockSpec((nseq, tt, D_MODEL), lambda b, t: (b, t, 0)),
        pl.BlockSpec((nseq, CONV_W - 1, D_CONV), lambda b, t: (b, 0, 0)),
        _const_spec((D_MODEL, D_IN_PAD)),
        _const_spec((CONV_W, D_CONV)),
        _const_spec((1, D_MODEL)),
        _const_spec((1, Q_RANK)),
        _const_spec((Q_RANK, N_HEADS * (QK_NOPE + QK_ROPE))),
        _const_spec((1, KV_RANK)),
        _const_spec((N_HEADS // 2, 2 * QK_NOPE, 2 * KV_RANK)),
        rope_spec, rope_spec, rope_spec,
    ]
    out_shape = [
        jax.ShapeDtypeStruct((B, T, D_CONV), BF16),
        q_shape,
        jax.ShapeDtypeStruct((B, T, D_QK), BF16),
        jax.ShapeDtypeStruct((B, T, KV_RANK), F32),
        jax.ShapeDtypeStruct((B, T, QK_ROPE), F32),
        jax.ShapeDtypeStruct((B, CONV_W - 1, D_CONV), F32),
    ]
    out_specs = [
        pl.BlockSpec((nseq, tt, D_CONV), lambda b, t: (b, t, 0)),
        q_spec,
        pl.BlockSpec((nseq, tt, D_QK), lambda b, t: (b, t, 0)),
        pl.BlockSpec((nseq, tt, KV_RANK), lambda b, t: (b, t, 0)),
        pl.BlockSpec((nseq, tt, QK_ROPE), lambda b, t: (b, t, 0)),
        pl.BlockSpec((nseq, CONV_W - 1, D_CONV), lambda b, t: (b, 0, 0)),
    ]
    if vt_tile:
        out_shape.append(jax.ShapeDtypeStruct((B, T // vt_tile, VT_ROWS, vt_tile), BF16))
        out_specs.append(pl.BlockSpec((1, tt // vt_tile, VT_ROWS, vt_tile), lambda b, t: (b, t, 0, 0)))
    return pl.pallas_call(
        functools.partial(_mix_in_kernel, nseq=nseq, tt=tt, vt_tile=vt_tile, qt_tile=qt_tile),
        grid=(nb, nt),
        in_specs=in_specs,
        out_specs=out_specs,
        out_shape=out_shape,
        scratch_shapes=[pltpu.VMEM((nseq, tt + HALO, D_CONV), F32)],
        compiler_params=pltpu.CompilerParams(
            dimension_semantics=("arbitrary", "arbitrary"), vmem_limit_bytes=VMEM_LIMIT_BYTES),
        name="mix_in",
    )(x, conv_state, lw["w1"], lw["w_conv"], lw["g_mix_pre"], lw["g_qa"], lw["w_uq"], lw["g_kva"],
      lw["w_ukp"], *ropes)


def _attend_prompt_kernel(q_ref, qn_ref, k_ref, vt_ref, o_ref, m_ref, acc_ref,
                          s0_ref, sa_ref, sb_ref, s0_max_ref, sa_max_ref, sb_max_ref, *, tq, tk):
    cols = N_HEADS * tq
    gw = cols // ATTN_COL_GROUPS
    qi = pl.program_id(1)
    q0 = qi * tq
    n_last = q0 // tk

    m_ref[...] = jnp.full(m_ref.shape, -jnp.inf, F32)
    acc_ref[...] = jnp.zeros(acc_ref.shape, F32)

    def scores(j, buf, query_ref=q_ref):
        s_ref, smax_ref = buf
        k0 = pl.multiple_of(j * tk, tk)
        s = _dot(k_ref[0, pl.ds(k0, tk), :], query_ref[0, 0])
        s_ref[...] = s
        smax_ref[...] = jnp.max(s, axis=0, keepdims=True)

    def consume(j, buf, masked):
        s_ref, smax_ref = buf
        vt = vt_ref[0, j]
        for g in range(ATTN_COL_GROUPS):
            sl = slice(gw * g, gw * (g + 1))
            s = s_ref[:, sl]
            if masked:
                k_pos = j * tk + lax.broadcasted_iota(jnp.int32, (tk, gw), 0)
                q_pos = q0 + (lax.broadcasted_iota(jnp.int32, (tk, gw), 1) & (tq - 1))
                s = jnp.where(_chunk_of(k_pos) <= _chunk_of(q_pos), s, -jnp.inf)
                s_max = jnp.max(s, axis=0, keepdims=True)
            else:
                s_max = smax_ref[:, sl]
            m_prev = m_ref[:, sl]
            m_new = jnp.maximum(m_prev, s_max)
            p = jnp.exp2(s - m_new).astype(BF16)
            acc_ref[:, sl] = jnp.exp2(m_prev - m_new) * acc_ref[:, sl] + _dot(vt, p)
            m_ref[:, sl] = m_new

    first = (s0_ref, s0_max_ref)
    bufs = ((sa_ref, sa_max_ref), (sb_ref, sb_max_ref))

    def score_next_query_tile():
        scores(0, first, qn_ref)

    def run(start, count):
        for t in range(count):
            scores(start + t + 1, bufs[(t + 1) % 2])
            consume(start + t, bufs[t % 2], False)

    @pl.when(qi == 0)
    def _():
        scores(0, first)

    @pl.when(n_last == 0)
    def _():
        consume(0, first, True)
        score_next_query_tile()

    @pl.when(n_last > 0)
    def _():
        scores(1, bufs[0])
        consume(0, first, False)

    later = jnp.maximum(n_last - 1, 0)

    def unrolled_body(i, carry):
        run(1 + ATTN_UNROLL * i, ATTN_UNROLL)
        return carry

    lax.fori_loop(0, later // ATTN_UNROLL, unrolled_body, 0)
    done = (later // ATTN_UNROLL) * ATTN_UNROLL
    for rest in range(ATTN_UNROLL):
        @pl.when((n_last > 0) & (later - done == rest))
        def _():
            run(1 + done, rest)
            score_next_query_tile()
            consume(n_last, bufs[rest % 2], True)

    o = acc_ref[0:KV_RANK, :] / acc_ref[KV_RANK:KV_RANK + 1, :]
    for h in range(N_HEADS):
        o_ref[0, :, KV_RANK * h:KV_RANK * (h + 1)] = o[:, tq * h:tq * (h + 1)].T.astype(BF16)


def _attend_prompt(q_t, kcat, vt):
    B, T, _ = kcat.shape
    tk = vt.shape[3]
    cols = q_t.shape[3]
    tq = cols // N_HEADS
    nq = T // tq
    assert tq % CHUNK == 0 and tq & (tq - 1) == 0 and T % tk == 0 and tk % tq == 0
    score_buf = pltpu.VMEM((tk, cols), F32)
    max_buf = pltpu.VMEM((1, cols), F32)
    return pl.pallas_call(
        functools.partial(_attend_prompt_kernel, tq=tq, tk=tk),
        grid=(B, nq),
        in_specs=[
            pl.BlockSpec((1, 1, D_QK, cols), lambda b, i: (b, i, 0, 0)),
            pl.BlockSpec((1, 1, D_QK, cols), lambda b, i: (b, jnp.minimum(i + 1, nq - 1), 0, 0)),
            pl.BlockSpec((1, T, D_QK), lambda b, i: (b, 0, 0)),
            pl.BlockSpec((1, T // tk, VT_ROWS, tk), lambda b, i: (b, 0, 0, 0)),
        ],
        out_specs=pl.BlockSpec((1, tq, N_HEADS * KV_RANK), lambda b, i: (b, i, 0)),
        out_shape=jax.ShapeDtypeStruct((B, T, N_HEADS * KV_RANK), BF16),
        scratch_shapes=[
            pltpu.VMEM((1, cols), F32),
            pltpu.VMEM((VT_ROWS, cols), F32),
            score_buf, score_buf, score_buf,
            max_buf, max_buf, max_buf,
        ],
        compiler_params=pltpu.CompilerParams(
            dimension_semantics=("arbitrary", "arbitrary"), vmem_limit_bytes=VMEM_LIMIT_BYTES),
        name="attend_prompt",
    )(q_t, q_t, kcat, vt)


def _attend_sample_kernel(q_ref, ckv_ref, kpe_ref, kn_ref, o_ref, *, ts, past):
    rows = N_HEADS * ts
    q = q_ref[0].reshape(rows, D_QK)
    ck = ckv_ref[0, 0].astype(BF16)
    kp_t = kpe_ref[0, 0].astype(BF16)
    kn = kn_ref[0]
    s_old = _dot_nt(q[:, 0:KV_RANK], ck) + _dot(q[:, KV_RANK:D_QK], kp_t)
    s_new = _dot_nt(q, kn)
    q_pos = past + (lax.broadcasted_iota(jnp.int32, (rows, ts), 0) & (ts - 1))
    k_pos = past + lax.broadcasted_iota(jnp.int32, (rows, ts), 1)
    s_new = jnp.where(_chunk_of(k_pos) <= _chunk_of(q_pos), s_new, -jnp.inf)
    m = jnp.maximum(jnp.max(s_old, axis=1, keepdims=True), jnp.max(s_new, axis=1, keepdims=True))
    p_old = jnp.exp2(s_old - m)
    p_new = jnp.exp2(s_new - m)
    l = jnp.sum(p_old, axis=1, keepdims=True) + jnp.sum(p_new, axis=1, keepdims=True)
    o = (_dot(p_old.astype(BF16), ck) + _dot(p_new.astype(BF16), kn[:, 0:KV_RANK])) / l
    for h in range(N_HEADS):
        o_ref[0, :, KV_RANK * h:KV_RANK * (h + 1)] = o[ts * h:ts * (h + 1), :].astype(BF16)


def _attend_sample(q, cache_ckv, cache_kpe_t, kcat, layer):
    B, _, ts, _ = q.shape
    past = cache_ckv.shape[2]
    return pl.pallas_call(
        functools.partial(_attend_sample_kernel, ts=ts, past=past),
        grid=(B,),
        in_specs=[
            pl.BlockSpec((1, N_HEADS, ts, D_QK), lambda b: (b, 0, 0, 0)),
            pl.BlockSpec((1, 1, past, KV_RANK), lambda b: (layer, b, 0, 0)),
            pl.BlockSpec((1, 1, QK_ROPE, past), lambda b: (layer, b, 0, 0)),
            pl.BlockSpec((1, ts, D_QK), lambda b: (b, 0, 0)),
        ],
        out_specs=pl.BlockSpec((1, ts, N_HEADS * KV_RANK), lambda b: (b, 0, 0)),
        out_shape=jax.ShapeDtypeStruct((B, ts, N_HEADS * KV_RANK), BF16),
        compiler_params=pltpu.CompilerParams(
            dimension_semantics=("arbitrary",), vmem_limit_bytes=VMEM_LIMIT_BYTES),
        name="attend_sample",
    )(q, cache_ckv, cache_kpe_t, kcat)


def _post_kernel(x_ref, yconv_ref, olat_ref, fs_ref, wuvg_ref, wo_ref, gpost_ref, gfpre_ref, wup_ref,
                 wfc_ref, bfc_ref, wdown_ref, gfpost_ref,
                 xo_ref, fso_ref,
                 ubuf_ref, vbuf_ref, acc_ref, hf_ref, *, nseq, tt):
    rows = nseq * tt

    @pl.when(pl.program_id(1) == 0)
    def _():
        fso_ref[...] = fs_ref[...]

    half = D_MODEL // 2
    ol = olat_ref[...].reshape(rows, N_HEADS * KV_RANK)
    for g in range(2):
        hf_ref[:, D_CONV + 4 * V_HEAD * g:D_CONV + 4 * V_HEAD * (g + 1)] = (
            _dot(ol[:, half * g:half * (g + 1)], wuvg_ref[g]).astype(BF16))
    hf_ref[:, 0:D_CONV] = yconv_ref[...].reshape(rows, D_CONV)
    mix = _dot(hf_ref[...], wo_ref[...])
    x1 = x_ref[...].reshape(rows, D_MODEL) + _rms(mix, gpost_ref[...])
    xo_ref[...] = x1.reshape(nseq, tt, D_MODEL)
    hf_ref[...] = _rms(x1, gfpre_ref[...]).astype(BF16)
    acc_ref[...] = jnp.zeros(acc_ref.shape, F32)

    def up_proj(o, w, buf_ref):
        for part in range(2):
            src = slice(D_FF * part + o, D_FF * part + o + w)
            dst = slice(FF_CHUNK * part, FF_CHUNK * part + w)
            _stage_halo(buf_ref, dst, fso_ref[:, :, src])
            up = _dot(hf_ref[...], wup_ref[:, src]).reshape(nseq, tt, w)
            buf_ref[:, HALO:HALO + tt, dst] = up
            fso_ref[:, :, src] = up[:, tt - 2:tt, :]

    def gate_down(o, w, buf_ref):
        u = []
        for part in range(2):
            src = slice(D_FF * part + o, D_FF * part + o + w)
            dst = slice(FF_CHUNK * part, FF_CHUNK * part + w)
            wc = wfc_ref[:, src]
            ext = buf_ref[:, :, dst]
            s1 = pltpu.roll(ext, 1, 1)[:, HALO:HALO + tt, :]
            s2 = pltpu.roll(ext, 2, 1)[:, HALO:HALO + tt, :]
            u.append((s2 * wc[0] + s1 * wc[1] + ext[:, HALO:HALO + tt, :] * wc[2] + bfc_ref[:, src]).reshape(rows, w))
        a, b = u
        gated = (a * (1.0 / (1.0 + jnp.exp(-a))) * b).astype(BF16)
        acc_ref[...] += _dot(gated, wdown_ref[o:o + w, :])

    bufs = (ubuf_ref, vbuf_ref)
    up_proj(*FF_SPLITS[0], bufs[0])
    for c, (o, w) in enumerate(FF_SPLITS):
        if c + 1 < len(FF_SPLITS):
            up_proj(*FF_SPLITS[c + 1], bufs[(c + 1) % 2])
        gate_down(o, w, bufs[c % 2])
    xo_ref[...] = xo_ref[...] + _rms(acc_ref[...], gfpost_ref[...]).reshape(nseq, tt, D_MODEL)


def _post(x, yconv, olat, ffn_state, lw, *, nseq, tt):
    B, T, _ = x.shape
    nb, nt = B // nseq, T // tt
    rows = nseq * tt
    state_block = (nseq, CONV_W - 1, 2 * D_FF)
    in_specs = [
        pl.BlockSpec((nseq, tt, D_MODEL), lambda b, t: (b, t, 0)),
        pl.BlockSpec((nseq, tt, D_CONV), lambda b, t: (b, t, 0)),
        pl.BlockSpec((nseq, tt, N_HEADS * KV_RANK), lambda b, t: (b, t, 0)),
        pl.BlockSpec(state_block, lambda b, t: (b, 0, 0), pipeline_mode=pl.Buffered(1)),
        _const_spec((2, 4 * KV_RANK, 4 * V_HEAD)),
        _const_spec((D_MODEL, D_MODEL)),
        _const_spec((1, D_MODEL)),
        _const_spec((1, D_MODEL)),
        _const_spec((D_MODEL, 2 * D_FF)),
        _const_spec((CONV_W, 2 * D_FF)),
        _const_spec((1, 2 * D_FF)),
        _const_spec((D_FF, D_MODEL)),
        _const_spec((1, D_MODEL)),
    ]
    return pl.pallas_call(
        functools.partial(_post_kernel, nseq=nseq, tt=tt),
        grid=(nb, nt),
        in_specs=in_specs,
        out_specs=(
            pl.BlockSpec((nseq, tt, D_MODEL), lambda b, t: (b, t, 0)),
            pl.BlockSpec(state_block, lambda b, t: (b, 0, 0)),
        ),
        out_shape=(
            jax.ShapeDtypeStruct((B, T, D_MODEL), F32),
            jax.ShapeDtypeStruct((B, CONV_W - 1, 2 * D_FF), F32),
        ),
        scratch_shapes=[
            pltpu.VMEM((nseq, tt + HALO, 2 * FF_CHUNK), F32),
            pltpu.VMEM((nseq, tt + HALO, 2 * FF_CHUNK), F32),
            pltpu.VMEM((rows, D_MODEL), F32),
            pltpu.VMEM((rows, D_MODEL), BF16),
        ],
        compiler_params=pltpu.CompilerParams(
            dimension_semantics=("arbitrary", "arbitrary"), vmem_limit_bytes=VMEM_LIMIT_BYTES),
        name="post_ffn",
    )(x, yconv, olat, ffn_state, lw["w_uvg"], lw["w_o"], lw["g_mix_post"], lw["g_ffn_pre"], lw["w_up"],
      lw["w_ffn_conv"], lw["b_ffn_conv"], lw["w_down"], lw["g_ffn_post"])


def _block_diag(blocks):
    n = len(blocks)
    rows = []
    for i, blk in enumerate(blocks):
        rows.append(jnp.concatenate(
            [blk if j == i else jnp.zeros((blk.shape[0], blocks[j].shape[1]), blk.dtype) for j in range(n)],
            axis=1))
    return jnp.concatenate(rows, axis=0)


def _layer_weights(l, w_in, w_conv, g_qa, w_uq, g_kva, w_uk, w_uv, w_o, g_mix_pre, g_mix_post,
                   w_up, w_ffn_conv, b_ffn_conv, w_down, g_ffn_pre, g_ffn_post):
    d_in = w_in.shape[2]
    uq = w_uq[l].reshape(Q_RANK, N_HEADS, QK_NOPE + QK_ROPE)
    uq = jnp.concatenate([uq[:, :, :QK_NOPE].reshape(Q_RANK, N_HEADS * QK_NOPE),
                          uq[:, :, QK_NOPE:].reshape(Q_RANK, N_HEADS * QK_ROPE)], axis=1)
    uk_t = jnp.transpose(w_uk[l], (1, 2, 0))
    uv = jnp.transpose(w_uv[l], (1, 0, 2))
    return {
        "w1": jnp.pad(w_in[l], ((0, 0), (0, D_IN_PAD - d_in))).astype(BF16),
        "w_conv": w_conv[l],
        "g_mix_pre": g_mix_pre[l][None],
        "g_qa": g_qa[l][None],
        "w_uq": uq.astype(BF16),
        "g_kva": g_kva[l][None],
        "w_ukp": jnp.stack([_block_diag([uk_t[2 * p], uk_t[2 * p + 1]])
                            for p in range(N_HEADS // 2)]).astype(BF16),
        "w_uvg": jnp.stack([_block_diag([uv[4 * g + j] for j in range(4)]) for g in range(2)]).astype(BF16),
        "w_o": w_o[l].astype(BF16),
        "g_mix_post": g_mix_post[l][None],
        "g_ffn_pre": g_ffn_pre[l][None],
        "w_up": w_up[l].astype(BF16),
        "w_ffn_conv": w_ffn_conv[l],
        "b_ffn_conv": b_ffn_conv[l][None],
        "w_down": w_down[l].astype(BF16),
        "g_ffn_post": g_ffn_post[l][None],
    }


def _rope_tables(pos, reps):
    inv = 1.0 / (ROPE_BASE ** (jnp.arange(0, QK_ROPE, 2, dtype=F32) / QK_ROPE))
    ang = pos.astype(F32)[:, None] * inv[None, :]
    cos, sin = jnp.cos(ang), jnp.sin(ang)
    zero = jnp.zeros_like(sin)
    group = LANES // QK_ROPE
    rc = jnp.tile(jnp.concatenate([cos, cos], axis=1), (reps, group))
    rs1 = jnp.tile(jnp.concatenate([-sin, zero], axis=1), (reps, group))
    rs2 = jnp.tile(jnp.concatenate([zero, sin], axis=1), (reps, group))
    return rc, rs1, rs2


def _pick_tile(T, target):
    tt = min(T, target)
    assert T % tt == 0 and tt % BF16_SUBLANES == 0
    return tt


def _trunk(x_prompt, x_sample, cache_ckv, cache_kpe, state_conv, state_ffn, weights):
    depth = cache_ckv.shape[0]
    Bp, Tp, _ = x_prompt.shape
    Bs, Ts, _ = x_sample.shape
    past = cache_ckv.shape[2]
    assert Ts >= CONV_W - 1 and Ts % BF16_SUBLANES == 0

    tt = _pick_tile(Tp, 512)
    tt_in = _pick_tile(Tp, 1024)
    tq = _pick_tile(Tp, 512)
    tk = _pick_tile(Tp, 512)
    ns = math.gcd(Bs, SAMPLE_SEQS_PER_STEP)
    p_ropes = _rope_tables(jnp.arange(Tp), 1)
    s_ropes = _rope_tables(past + jnp.arange(Ts), ns)
    p_conv0 = jnp.zeros((Bp, CONV_W - 1, D_CONV), F32)
    p_ffn0 = jnp.zeros((Bp, CONV_W - 1, 2 * D_FF), F32)

    cache_kpe_t = jnp.swapaxes(cache_kpe, 2, 3)
    xp, xs = x_prompt, x_sample
    outs = [[] for _ in range(8)]
    for l in range(depth):
        lw = _layer_weights(l, *weights)
        yc_p, q_p, kcat_p, ckv_p, kpe_p, conv_p, vt_p = _mix_in(
            xp, p_conv0, lw, p_ropes, nseq=1, tt=tt_in, vt_tile=tk, qt_tile=tq)
        yc_s, q_s, kcat_s, ckv_s, kpe_s, conv_s = _mix_in(xs, state_conv[l], lw, s_ropes, nseq=ns, tt=Ts)
        ol_p = _attend_prompt(q_p, kcat_p, vt_p)
        ol_s = _attend_sample(q_s, cache_ckv, cache_kpe_t, kcat_s, l)
        xp, ffn_p = _post(xp, yc_p, ol_p, p_ffn0, lw, nseq=1, tt=tt)
        xs, ffn_s = _post(xs, yc_s, ol_s, state_ffn[l], lw, nseq=ns, tt=Ts)
        for acc, v in zip(outs, (ckv_p, kpe_p, conv_p, ffn_p, ckv_s, kpe_s, conv_s, ffn_s)):
            acc.append(v)
    return (xp, xs) + tuple(jnp.stack(v) for v in outs)


def kernel(x_prompt, x_sample, cache_ckv, cache_kpe, state_conv, state_ffn, w_in, w_conv, g_qa, w_uq, g_kva, w_uk, w_uv, w_o, g_mix_pre, g_mix_post, w_up, w_ffn_conv, b_ffn_conv, w_down, g_ffn_pre, g_ffn_post):
    weights = (w_in, w_conv, g_qa, w_uq, g_kva, w_uk, w_uv, w_o, g_mix_pre, g_mix_post,
               w_up, w_ffn_conv, b_ffn_conv, w_down, g_ffn_pre, g_ffn_post)
    return _trunk(x_prompt, x_sample, cache_ckv, cache_kpe, state_conv, state_ffn, weights)
```

```python
import functools
import math

import jax
import jax.numpy as jnp
from jax import lax
from jax.experimental import pallas as pl
from jax.experimental.pallas import tpu as pltpu

D_MODEL = 1024
CHUNK = 64
CONV_W = 3
D_CONV = D_MODEL // 2
N_HEADS = 8
QK_NOPE = 64
QK_ROPE = 32
V_HEAD = 64
KV_RANK = 128
Q_RANK = 256
D_FF = 2816
ROPE_BASE = 10000.0
EPS = 1e-6
ATTN_SCALE = 1.0 / math.sqrt(QK_NOPE + QK_ROPE)
Q_SCALE = ATTN_SCALE * math.log2(math.e)

LANES = 128
SUBLANES = 8
BF16_SUBLANES = 16
MXU_TILE = 256
VMEM_LIMIT_BYTES = 56 * 1024 * 1024

D_QK = KV_RANK + QK_ROPE
D_IN_PAD = 2048
COL_QA = 3 * D_CONV
COL_KV = COL_QA + Q_RANK
FF_SPLITS = ((0, 1280), (1280, 1536))
FF_CHUNK = max(w for _, w in FF_SPLITS)
assert sum(w for _, w in FF_SPLITS) == D_FF and all(o % MXU_TILE == 0 for o, _ in FF_SPLITS)
ATTN_UNROLL = 4
ATTN_COL_GROUPS = 1
VT_ROWS = KV_RANK + BF16_SUBLANES
HALO = SUBLANES
SAMPLE_SEQS_PER_STEP = 16

BF16 = jnp.bfloat16
F32 = jnp.float32


def _rms(x, g):
    return x * lax.rsqrt(jnp.mean(x * x, axis=-1, keepdims=True) + EPS) * g


def _dot(a, b):
    return jnp.dot(a, b, preferred_element_type=F32)


def _dot_nt(a, b):
    return lax.dot_general(a, b, (((1,), (1,)), ((), ())), preferred_element_type=F32)


def _chunk_of(pos):
    return lax.shift_right_logical(pos, CHUNK.bit_length() - 1)


def _rope(x, rc, rs1, rs2):
    return x * rc + pltpu.roll(x, LANES - QK_ROPE // 2, 1) * rs1 + pltpu.roll(x, QK_ROPE // 2, 1) * rs2


def _stage_halo(buf_ref, cols, prev):
    n_prev = prev.shape[1]
    buf_ref[:, 0:HALO - n_prev, cols] = jnp.zeros((prev.shape[0], HALO - n_prev, prev.shape[2]), prev.dtype)
    buf_ref[:, HALO - n_prev:HALO, cols] = prev


def _causal_conv(buf_ref, cur, prev, w):
    tt = cur.shape[1]
    _stage_halo(buf_ref, slice(None), prev)
    buf_ref[:, HALO:HALO + tt, :] = cur
    ext = buf_ref[...]
    s1 = pltpu.roll(ext, 1, 1)[:, HALO:HALO + tt, :]
    s2 = pltpu.roll(ext, 2, 1)[:, HALO:HALO + tt, :]
    return s2 * w[0] + s1 * w[1] + cur * w[2]


def _mix_in_kernel(x_ref, cs_ref, w1_ref, wconv_ref, gpre_ref, gqa_ref, wuq_ref, gkva_ref, wukp_ref,
                   rc_ref, rs1_ref, rs2_ref,
                   yconv_ref, q_ref, kcat_ref, ckv_ref, kpe_ref, cso_ref, *rest, nseq, tt, vt_tile, qt_tile):
    cbuf_ref = rest[-1]
    rows = nseq * tt

    @pl.when(pl.program_id(1) == 0)
    def _():
        cso_ref[...] = cs_ref[...]

    x = x_ref[...].reshape(rows, D_MODEL)
    h = _rms(x, gpre_ref[...]).astype(BF16)

    qa = _dot(h, w1_ref[:, COL_QA:COL_KV])
    kv = _dot(h, w1_ref[:, COL_KV:D_IN_PAD])
    xv = _dot(h, w1_ref[:, 0:D_CONV])
    gc = _dot(h, w1_ref[:, 2 * D_CONV:3 * D_CONV])
    gb = _dot(h, w1_ref[:, D_CONV:2 * D_CONV])
    qn = _rms(qa, gqa_ref[...]).astype(BF16)
    q = _dot(qn, wuq_ref[...])
    pair = 2 * QK_NOPE
    qls = [_dot(q[:, pair * p:pair * (p + 1)].astype(BF16), wukp_ref[p]) * Q_SCALE for p in range(N_HEADS // 2)]

    c = (gc * xv).reshape(nseq, tt, D_CONV)
    conv = _causal_conv(cbuf_ref, c, cso_ref[...], wconv_ref[...])
    yconv_ref[...] = (gb.reshape(nseq, tt, D_CONV) * conv).astype(BF16)
    cso_ref[...] = c[:, tt - 2:tt, :]

    rc = rc_ref[...]
    rs1 = rs1_ref[...]
    rs2 = rs2_ref[...]

    n_nope = N_HEADS * QK_NOPE
    heads_per_tile = LANES // QK_ROPE
    for p in range(N_HEADS // 2):
        ql = qls[p]
        for j in range(2):
            hd = 2 * p + j
            qh = ql[:, KV_RANK * j:KV_RANK * (j + 1)]
            if qt_tile:
                qh_t = qh.T.astype(BF16)
                for i in range(tt // qt_tile):
                    q_ref[0, i, 0:KV_RANK, qt_tile * hd:qt_tile * (hd + 1)] = (
                        qh_t[:, qt_tile * i:qt_tile * (i + 1)])
            else:
                q_ref[:, hd, :, 0:KV_RANK] = qh.reshape(nseq, tt, KV_RANK).astype(BF16)
    for half in range(N_HEADS // heads_per_tile):
        qp = _rope(q[:, n_nope + LANES * half:n_nope + LANES * (half + 1)], rc, rs1, rs2) * Q_SCALE
        qp_t = qp.T.astype(BF16) if qt_tile else None
        for j in range(heads_per_tile):
            hd = heads_per_tile * half + j
            if qt_tile:
                for i in range(tt // qt_tile):
                    q_ref[0, i, KV_RANK:D_QK, qt_tile * hd:qt_tile * (hd + 1)] = (
                        qp_t[QK_ROPE * j:QK_ROPE * (j + 1), qt_tile * i:qt_tile * (i + 1)])
            else:
                q_ref[:, hd, :, KV_RANK:D_QK] = (
                    qp[:, QK_ROPE * j:QK_ROPE * (j + 1)].reshape(nseq, tt, QK_ROPE).astype(BF16))

    ckv = _rms(kv[:, 0:KV_RANK], gkva_ref[...])
    kpe = _rope(kv[:, KV_RANK:2 * KV_RANK], rc, rs1, rs2)[:, 0:QK_ROPE]
    ckv_ref[...] = ckv.reshape(nseq, tt, KV_RANK)
    kpe_ref[...] = kpe.reshape(nseq, tt, QK_ROPE)
    kcat_ref[:, :, 0:KV_RANK] = ckv.reshape(nseq, tt, KV_RANK).astype(BF16)
    kcat_ref[:, :, KV_RANK:D_QK] = kpe.reshape(nseq, tt, QK_ROPE).astype(BF16)
    if vt_tile:
        vt_ref = rest[0]
        for i in range(tt // vt_tile):
            vt_ref[0, i, 0:KV_RANK, :] = ckv[vt_tile * i:vt_tile * (i + 1), :].T.astype(BF16)
            vt_ref[0, i, KV_RANK:VT_ROWS, :] = jnp.ones((VT_ROWS - KV_RANK, vt_tile), BF16)


def _const_spec(shape):
    nd = len(shape)
    return pl.BlockSpec(shape, lambda *_: (0,) * nd, pipeline_mode=pl.Buffered(1))


def _mix_in(x, conv_state, lw, ropes, *, nseq, tt, vt_tile=0, qt_tile=0):
    B, T, _ = x.shape
    assert not vt_tile or (nseq == 1 and tt % vt_tile == 0)
    assert not qt_tile or (nseq == 1 and tt % qt_tile == 0)
    if qt_tile:
        q_shape = jax.ShapeDtypeStruct((B, T // qt_tile, D_QK, N_HEADS * qt_tile), BF16)
        q_spec = pl.BlockSpec((1, tt // qt_tile, D_QK, N_HEADS * qt_tile), lambda b, t: (b, t, 0, 0))
    else:
        q_shape = jax.ShapeDtypeStruct((B, N_HEADS, T, D_QK), BF16)
        q_spec = pl.BlockSpec((nseq, N_HEADS, tt, D_QK), lambda b, t: (b, 0, t, 0))
    nb, nt = B // nseq, T // tt
    rows = nseq * tt
    rope_spec = pl.BlockSpec((rows, LANES), lambda b, t: (t, 0))
    in_specs = [
        pl.BlockSpec((nseq, tt, D_MODEL), lambda b, t: (b, t, 0)),
        pl.BlockSpec((nseq, CONV_W - 1, D_CONV), lambda b, t: (b, 0, 0)),
        _const_spec((D_MODEL, D_IN_PAD)),
        _const_spec((CONV_W, D_CONV)),
        _const_spec((1, D_MODEL)),
        _const_spec((1, Q_RANK)),
        _const_spec((Q_RANK, N_HEADS * (QK_NOPE + QK_ROPE))),
        _const_spec((1, KV_RANK)),
        _const_spec((N_HEADS // 2, 2 * QK_NOPE, 2 * KV_RANK)),
        rope_spec, rope_spec, rope_spec,
    ]
    out_shape = [
        jax.ShapeDtypeStruct((B, T, D_CONV), BF16),
        q_shape,
        jax.ShapeDtypeStruct((B, T, D_QK), BF16),
        jax.ShapeDtypeStruct((B, T, KV_RANK), F32),
        jax.ShapeDtypeStruct((B, T, QK_ROPE), F32),
        jax.ShapeDtypeStruct((B, CONV_W - 1, D_CONV), F32),
    ]
    out_specs = [
        pl.BlockSpec((nseq, tt, D_CONV), lambda b, t: (b, t, 0)),
        q_spec,
        pl.BlockSpec((nseq, tt, D_QK), lambda b, t: (b, t, 0)),
        pl.BlockSpec((nseq, tt, KV_RANK), lambda b, t: (b, t, 0)),
        pl.BlockSpec((nseq, tt, QK_ROPE), lambda b, t: (b, t, 0)),
        pl.BlockSpec((nseq, CONV_W - 1, D_CONV), lambda b, t: (b, 0, 0)),
    ]
    if vt_tile:
        out_shape.append(jax.ShapeDtypeStruct((B, T // vt_tile, VT_ROWS, vt_tile), BF16))
        out_specs.append(pl.BlockSpec((1, tt // vt_tile, VT_ROWS, vt_tile), lambda b, t: (b, t, 0, 0)))
    return pl.pallas_call(
        functools.partial(_mix_in_kernel, nseq=nseq, tt=tt, vt_tile=vt_tile, qt_tile=qt_tile),
        grid=(nb, nt),
        in_specs=in_specs,
        out_specs=out_specs,
        out_shape=out_shape,
        scratch_shapes=[pltpu.VMEM((nseq, tt + HALO, D_CONV), F32)],
        compiler_params=pltpu.CompilerParams(
            dimension_semantics=("arbitrary", "arbitrary"), vmem_limit_bytes=VMEM_LIMIT_BYTES),
        name="mix_in",
    )(x, conv_state, lw["w1"], lw["w_conv"], lw["g_mix_pre"], lw["g_qa"], lw["w_uq"], lw["g_kva"],
      lw["w_ukp"], *ropes)


def _attend_prompt_kernel(q_ref, qn_ref, k_ref, vt_ref, o_ref, m_ref, acc_ref,
                          s0_ref, sa_ref, sb_ref, s0_max_ref, sa_max_ref, sb_max_ref, *, tq, tk):
    cols = N_HEADS * tq
    gw = cols // ATTN_COL_GROUPS
    qi = pl.program_id(1)
    q0 = qi * tq
    n_last = q0 // tk

    m_ref[...] = jnp.full(m_ref.shape, -jnp.inf, F32)
    acc_ref[...] = jnp.zeros(acc_ref.shape, F32)

    def scores(j, buf, query_ref=q_ref):
        s_ref, smax_ref = buf
        k0 = pl.multiple_of(j * tk, tk)
        s = _dot(k_ref[0, pl.ds(k0, tk), :], query_ref[0, 0])
        s_ref[...] = s
        smax_ref[...] = jnp.max(s, axis=0, keepdims=True)

    def consume(j, buf, masked):
        s_ref, smax_ref = buf
        vt = vt_ref[0, j]
        for g in range(ATTN_COL_GROUPS):
            sl = slice(gw * g, gw * (g + 1))
            s = s_ref[:, sl]
            if masked:
                k_pos = j * tk + lax.broadcasted_iota(jnp.int32, (tk, gw), 0)
                q_pos = q0 + (lax.broadcasted_iota(jnp.int32, (tk, gw), 1) & (tq - 1))
                s = jnp.where(_chunk_of(k_pos) <= _chunk_of(q_pos), s, -jnp.inf)
                s_max = jnp.max(s, axis=0, keepdims=True)
            else:
                s_max = smax_ref[:, sl]
            m_prev = m_ref[:, sl]
            m_new = jnp.maximum(m_prev, s_max)
            p = jnp.exp2(s - m_new).astype(BF16)
            acc_ref[:, sl] = jnp.exp2(m_prev - m_new) * acc_ref[:, sl] + _dot(vt, p)
            m_ref[:, sl] = m_new

    first = (s0_ref, s0_max_ref)
    bufs = ((sa_ref, sa_max_ref), (sb_ref, sb_max_ref))

    def score_next_query_tile():
        scores(0, first, qn_ref)

    def run(start, count):
        for t in range(count):
            scores(start + t + 1, bufs[(t + 1) % 2])
            consume(start + t, bufs[t % 2], False)

    @pl.when(qi == 0)
    def _():
        scores(0, first)

    @pl.when(n_last == 0)
    def _():
        consume(0, first, True)
        score_next_query_tile()

    @pl.when(n_last > 0)
    def _():
        scores(1, bufs[0])
        consume(0, first, False)

    later = jnp.maximum(n_last - 1, 0)

    def unrolled_body(i, carry):
        run(1 + ATTN_UNROLL * i, ATTN_UNROLL)
        return carry

    lax.fori_loop(0, later // ATTN_UNROLL, unrolled_body, 0)
    done = (later // ATTN_UNROLL) * ATTN_UNROLL
    for rest in range(ATTN_UNROLL):
        @pl.when((n_last > 0) & (later - done == rest))
        def _():
            run(1 + done, rest)
            score_next_query_tile()
            consume(n_last, bufs[rest % 2], True)

    o = acc_ref[0:KV_RANK, :] / acc_ref[KV_RANK:KV_RANK + 1, :]
    for h in range(N_HEADS):
        o_ref[0, :, KV_RANK * h:KV_RANK * (h + 1)] = o[:, tq * h:tq * (h + 1)].T.astype(BF16)


def _attend_prompt(q_t, kcat, vt):
    B, T, _ = kcat.shape
    tk = vt.shape[3]
    cols = q_t.shape[3]
    tq = cols // N_HEADS
    nq = T // tq
    assert tq % CHUNK == 0 and tq & (tq - 1) == 0 and T % tk == 0 and tk % tq == 0
    score_buf = pltpu.VMEM((tk, cols), F32)
    max_buf = pltpu.VMEM((1, cols), F32)
    return pl.pallas_call(
        functools.partial(_attend_prompt_kernel, tq=tq, tk=tk),
        grid=(B, nq),
        in_specs=[
            pl.BlockSpec((1, 1, D_QK, cols), lambda b, i: (b, i, 0, 0)),
            pl.BlockSpec((1, 1, D_QK, cols), lambda b, i: (b, jnp.minimum(i + 1, nq - 1), 0, 0)),
            pl.BlockSpec((1, T, D_QK), lambda b, i: (b, 0, 0)),
            pl.BlockSpec((1, T // tk, VT_ROWS, tk), lambda b, i: (b, 0, 0, 0)),
        ],
        out_specs=pl.BlockSpec((1, tq, N_HEADS * KV_RANK), lambda b, i: (b, i, 0)),
        out_shape=jax.ShapeDtypeStruct((B, T, N_HEADS * KV_RANK), BF16),
        scratch_shapes=[
            pltpu.VMEM((1, cols), F32),
            pltpu.VMEM((VT_ROWS, cols), F32),
            score_buf, score_buf, score_buf,
            max_buf, max_buf, max_buf,
        ],
        compiler_params=pltpu.CompilerParams(
            dimension_semantics=("arbitrary", "arbitrary"), vmem_limit_bytes=VMEM_LIMIT_BYTES),
        name="attend_prompt",
    )(q_t, q_t, kcat, vt)


def _attend_sample_kernel(q_ref, ckv_ref, kpe_ref, kn_ref, o_ref, *, ts, past):
    rows = N_HEADS * ts
    q = q_ref[0].reshape(rows, D_QK)
    ck = ckv_ref[0, 0].astype(BF16)
    kp_t = kpe_ref[0, 0].astype(BF16)
    kn = kn_ref[0]
    s_old = _dot_nt(q[:, 0:KV_RANK], ck) + _dot(q[:, KV_RANK:D_QK], kp_t)
    s_new = _dot_nt(q, kn)
    q_pos = past + (lax.broadcasted_iota(jnp.int32, (rows, ts), 0) & (ts - 1))
    k_pos = past + lax.broadcasted_iota(jnp.int32, (rows, ts), 1)
    s_new = jnp.where(_chunk_of(k_pos) <= _chunk_of(q_pos), s_new, -jnp.inf)
    m = jnp.maximum(jnp.max(s_old, axis=1, keepdims=True), jnp.max(s_new, axis=1, keepdims=True))
    p_old = jnp.exp2(s_old - m)
    p_new = jnp.exp2(s_new - m)
    l = jnp.sum(p_old, axis=1, keepdims=True) + jnp.sum(p_new, axis=1, keepdims=True)
    o = (_dot(p_old.astype(BF16), ck) + _dot(p_new.astype(BF16), kn[:, 0:KV_RANK])) / l
    for h in range(N_HEADS):
        o_ref[0, :, KV_RANK * h:KV_RANK * (h + 1)] = o[ts * h:ts * (h + 1), :].astype(BF16)


def _attend_sample(q, cache_ckv, cache_kpe_t, kcat, layer):
    B, _, ts, _ = q.shape
    past = cache_ckv.shape[2]
    return pl.pallas_call(
        functools.partial(_attend_sample_kernel, ts=ts, past=past),
        grid=(B,),
        in_specs=[
            pl.BlockSpec((1, N_HEADS, ts, D_QK), lambda b: (b, 0, 0, 0)),
            pl.BlockSpec((1, 1, past, KV_RANK), lambda b: (layer, b, 0, 0)),
            pl.BlockSpec((1, 1, QK_ROPE, past), lambda b: (layer, b, 0, 0)),
            pl.BlockSpec((1, ts, D_QK), lambda b: (b, 0, 0)),
        ],
        out_specs=pl.BlockSpec((1, ts, N_HEADS * KV_RANK), lambda b: (b, 0, 0)),
        out_shape=jax.ShapeDtypeStruct((B, ts, N_HEADS * KV_RANK), BF16),
        compiler_params=pltpu.CompilerParams(
            dimension_semantics=("arbitrary",), vmem_limit_bytes=VMEM_LIMIT_BYTES),
        name="attend_sample",
    )(q, cache_ckv, cache_kpe_t, kcat)


def _post_kernel(x_ref, yconv_ref, olat_ref, fs_ref, wuvg_ref, wo_ref, gpost_ref, gfpre_ref, wup_ref,
                 wfc_ref, bfc_ref, wdown_ref, gfpost_ref,
                 xo_ref, fso_ref,
                 ubuf_ref, vbuf_ref, acc_ref, hf_ref, *, nseq, tt):
    rows = nseq * tt

    @pl.when(pl.program_id(1) == 0)
    def _():
        fso_ref[...] = fs_ref[...]

    half = D_MODEL // 2
    ol = olat_ref[...].reshape(rows, N_HEADS * KV_RANK)
    for g in range(2):
        hf_ref[:, D_CONV + 4 * V_HEAD * g:D_CONV + 4 * V_HEAD * (g + 1)] = (
            _dot(ol[:, half * g:half * (g + 1)], wuvg_ref[g]).astype(BF16))
    hf_ref[:, 0:D_CONV] = yconv_ref[...].reshape(rows, D_CONV)
    mix = _dot(hf_ref[...], wo_ref[...])
    x1 = x_ref[...].reshape(rows, D_MODEL) + _rms(mix, gpost_ref[...])
    xo_ref[...] = x1.reshape(nseq, tt, D_MODEL)
    hf_ref[...] = _rms(x1, gfpre_ref[...]).astype(BF16)
    acc_ref[...] = jnp.zeros(acc_ref.shape, F32)

    def up_proj(o, w, buf_ref):
        for part in range(2):
            src = slice(D_FF * part + o, D_FF * part + o + w)
            dst = slice(FF_CHUNK * part, FF_CHUNK * part + w)
            _stage_halo(buf_ref, dst, fso_ref[:, :, src])
            up = _dot(hf_ref[...], wup_ref[:, src]).reshape(nseq, tt, w)
            buf_ref[:, HALO:HALO + tt, dst] = up
            fso_ref[:, :, src] = up[:, tt - 2:tt, :]

    def gate_down(o, w, buf_ref):
        u = []
        for part in range(2):
            src = slice(D_FF * part + o, D_FF * part + o + w)
            dst = slice(FF_CHUNK * part, FF_CHUNK * part + w)
            wc = wfc_ref[:, src]
            ext = buf_ref[:, :, dst]
            s1 = pltpu.roll(ext, 1, 1)[:, HALO:HALO + tt, :]
            s2 = pltpu.roll(ext, 2, 1)[:, HALO:HALO + tt, :]
            u.append((s2 * wc[0] + s1 * wc[1] + ext[:, HALO:HALO + tt, :] * wc[2] + bfc_ref[:, src]).reshape(rows, w))
        a, b = u
        gated = (a * (1.0 / (1.0 + jnp.exp(-a))) * b).astype(BF16)
        acc_ref[...] += _dot(gated, wdown_ref[o:o + w, :])

    bufs = (ubuf_ref, vbuf_ref)
    up_proj(*FF_SPLITS[0], bufs[0])
    for c, (o, w) in enumerate(FF_SPLITS):
        if c + 1 < len(FF_SPLITS):
            up_proj(*FF_SPLITS[c + 1], bufs[(c + 1) % 2])
        gate_down(o, w, bufs[c % 2])
    xo_ref[...] = xo_ref[...] + _rms(acc_ref[...], gfpost_ref[...]).reshape(nseq, tt, D_MODEL)


def _post(x, yconv, olat, ffn_state, lw, *, nseq, tt):
    B, T, _ = x.shape
    nb, nt = B // nseq, T // tt
    rows = nseq * tt
    state_block = (nseq, CONV_W - 1, 2 * D_FF)
    in_specs = [
        pl.BlockSpec((nseq, tt, D_MODEL), lambda b, t: (b, t, 0)),
        pl.BlockSpec((nseq, tt, D_CONV), lambda b, t: (b, t, 0)),
        pl.BlockSpec((nseq, tt, N_HEADS * KV_RANK), lambda b, t: (b, t, 0)),
        pl.BlockSpec(state_block, lambda b, t: (b, 0, 0), pipeline_mode=pl.Buffered(1)),
        _const_spec((2, 4 * KV_RANK, 4 * V_HEAD)),
        _const_spec((D_MODEL, D_MODEL)),
        _const_spec((1, D_MODEL)),
        _const_spec((1, D_MODEL)),
        _const_spec((D_MODEL, 2 * D_FF)),
        _const_spec((CONV_W, 2 * D_FF)),
        _const_spec((1, 2 * D_FF)),
        _const_spec((D_FF, D_MODEL)),
        _const_spec((1, D_MODEL)),
    ]
    return pl.pallas_call(
        functools.partial(_post_kernel, nseq=nseq, tt=tt),
        grid=(nb, nt),
        in_specs=in_specs,
        out_specs=(
            pl.BlockSpec((nseq, tt, D_MODEL), lambda b, t: (b, t, 0)),
            pl.BlockSpec(state_block, lambda b, t: (b, 0, 0)),
        ),
        out_shape=(
            jax.ShapeDtypeStruct((B, T, D_MODEL), F32),
            jax.ShapeDtypeStruct((B, CONV_W - 1, 2 * D_FF), F32),
        ),
        scratch_shapes=[
            pltpu.VMEM((nseq, tt + HALO, 2 * FF_CHUNK), F32),
            pltpu.VMEM((nseq, tt + HALO, 2 * FF_CHUNK), F32),
            pltpu.VMEM((rows, D_MODEL), F32),
            pltpu.VMEM((rows, D_MODEL), BF16),
        ],
        compiler_params=pltpu.CompilerParams(
            dimension_semantics=("arbitrary", "arbitrary"), vmem_limit_bytes=VMEM_LIMIT_BYTES),
        name="post_ffn",
    )(x, yconv, olat, ffn_state, lw["w_uvg"], lw["w_o"], lw["g_mix_post"], lw["g_ffn_pre"], lw["w_up"],
      lw["w_ffn_conv"], lw["b_ffn_conv"], lw["w_down"], lw["g_ffn_post"])


def _block_diag(blocks):
    n = len(blocks)
    rows = []
    for i, blk in enumerate(blocks):
        rows.append(jnp.concatenate(
            [blk if j == i else jnp.zeros((blk.shape[0], blocks[j].shape[1]), blk.dtype) for j in range(n)],
            axis=1))
    return jnp.concatenate(rows, axis=0)


def _layer_weights(l, w_in, w_conv, g_qa, w_uq, g_kva, w_uk, w_uv, w_o, g_mix_pre, g_mix_post,
                   w_up, w_ffn_conv, b_ffn_conv, w_down, g_ffn_pre, g_ffn_post):
    d_in = w_in.shape[2]
    uq = w_uq[l].reshape(Q_RANK, N_HEADS, QK_NOPE + QK_ROPE)
    uq = jnp.concatenate([uq[:, :, :QK_NOPE].reshape(Q_RANK, N_HEADS * QK_NOPE),
                          uq[:, :, QK_NOPE:].reshape(Q_RANK, N_HEADS * QK_ROPE)], axis=1)
    uk_t = jnp.transpose(w_uk[l], (1, 2, 0))
    uv = jnp.transpose(w_uv[l], (1, 0, 2))
    return {
        "w1": jnp.pad(w_in[l], ((0, 0), (0, D_IN_PAD - d_in))).astype(BF16),
        "w_conv": w_conv[l],
        "g_mix_pre": g_mix_pre[l][None],
        "g_qa": g_qa[l][None],
        "w_uq": uq.astype(BF16),
        "g_kva": g_kva[l][None],
        "w_ukp": jnp.stack([_block_diag([uk_t[2 * p], uk_t[2 * p + 1]])
                            for p in range(N_HEADS // 2)]).astype(BF16),
        "w_uvg": jnp.stack([_block_diag([uv[4 * g + j] for j in range(4)]) for g in range(2)]).astype(BF16),
        "w_o": w_o[l].astype(BF16),
        "g_mix_post": g_mix_post[l][None],
        "g_ffn_pre": g_ffn_pre[l][None],
        "w_up": w_up[l].astype(BF16),
        "w_ffn_conv": w_ffn_conv[l],
        "b_ffn_conv": b_ffn_conv[l][None],
        "w_down": w_down[l].astype(BF16),
        "g_ffn_post": g_ffn_post[l][None],
    }


def _rope_tables(pos, reps):
    inv = 1.0 / (ROPE_BASE ** (jnp.arange(0, QK_ROPE, 2, dtype=F32) / QK_ROPE))
    ang = pos.astype(F32)[:, None] * inv[None, :]
    cos, sin = jnp.cos(ang), jnp.sin(ang)
    zero = jnp.zeros_like(sin)
    group = LANES // QK_ROPE
    rc = jnp.tile(jnp.concatenate([cos, cos], axis=1), (reps, group))
    rs1 = jnp.tile(jnp.concatenate([-sin, zero], axis=1), (reps, group))
    rs2 = jnp.tile(jnp.concatenate([zero, sin], axis=1), (reps, group))
    return rc, rs1, rs2


def _pick_tile(T, target):
    tt = min(T, target)
    assert T % tt == 0 and tt % BF16_SUBLANES == 0
    return tt


def _trunk(x_prompt, x_sample, cache_ckv, cache_kpe, state_conv, state_ffn, weights):
    depth = cache_ckv.shape[0]
    Bp, Tp, _ = x_prompt.shape
    Bs, Ts, _ = x_sample.shape
    past = cache_ckv.shape[2]
    assert Ts >= CONV_W - 1 and Ts % BF16_SUBLANES == 0

    tt = _pick_tile(Tp, 512)
    tt_in = _pick_tile(Tp, 1024)
    tq = _pick_tile(Tp, 256)
    tk = _pick_tile(Tp, 512)
    ns = math.gcd(Bs, SAMPLE_SEQS_PER_STEP)
    p_ropes = _rope_tables(jnp.arange(Tp), 1)
    s_ropes = _rope_tables(past + jnp.arange(Ts), ns)
    p_conv0 = jnp.zeros((Bp, CONV_W - 1, D_CONV), F32)
    p_ffn0 = jnp.zeros((Bp, CONV_W - 1, 2 * D_FF), F32)

    cache_kpe_t = jnp.swapaxes(cache_kpe, 2, 3)
    xp, xs = x_prompt, x_sample
    outs = [[] for _ in range(8)]
    for l in range(depth):
        lw = _layer_weights(l, *weights)
        yc_p, q_p, kcat_p, ckv_p, kpe_p, conv_p, vt_p = _mix_in(
            xp, p_conv0, lw, p_ropes, nseq=1, tt=tt_in, vt_tile=tk, qt_tile=tq)
        yc_s, q_s, kcat_s, ckv_s, kpe_s, conv_s = _mix_in(xs, state_conv[l], lw, s_ropes, nseq=ns, tt=Ts)
        ol_p = _attend_prompt(q_p, kcat_p, vt_p)
        ol_s = _attend_sample(q_s, cache_ckv, cache_kpe_t, kcat_s, l)
        xp, ffn_p = _post(xp, yc_p, ol_p, p_ffn0, lw, nseq=1, tt=tt)
        xs, ffn_s = _post(xs, yc_s, ol_s, state_ffn[l], lw, nseq=ns, tt=Ts)
        for acc, v in zip(outs, (ckv_p, kpe_p, conv_p, ffn_p, ckv_s, kpe_s, conv_s, ffn_s)):
            acc.append(v)
    return (xp, xs) + tuple(jnp.stack(v) for v in outs)


def kernel(x_prompt, x_sample, cache_ckv, cache_kpe, state_conv, state_ffn, w_in, w_conv, g_qa, w_uq, g_kva, w_uk, w_uv, w_o, g_mix_pre, g_mix_post, w_up, w_ffn_conv, b_ffn_conv, w_down, g_ffn_pre, g_ffn_post):
    weights = (w_in, w_conv, g_qa, w_uq, g_kva, w_uk, w_uv, w_o, g_mix_pre, g_mix_post,
               w_up, w_ffn_conv, b_ffn_conv, w_down, g_ffn_pre, g_ffn_post)
    return _trunk(x_prompt, x_sample, cache_ckv, cache_kpe, state_conv, state_ffn, weights)
```

```python
import functools
import math

import jax
import jax.numpy as jnp
from jax import lax
from jax.experimental import pallas as pl
from jax.experimental.pallas import tpu as pltpu

D_MODEL = 1024
CHUNK = 64
CONV_W = 3
D_CONV = D_MODEL // 2
N_HEADS = 8
QK_NOPE = 64
QK_ROPE = 32
V_HEAD = 64
KV_RANK = 128
Q_RANK = 256
D_FF = 2816
ROPE_BASE = 10000.0
EPS = 1e-6
ATTN_SCALE = 1.0 / math.sqrt(QK_NOPE + QK_ROPE)
Q_SCALE = ATTN_SCALE * math.log2(math.e)

LANES = 128
SUBLANES = 8
BF16_SUBLANES = 16
MXU_TILE = 256
VMEM_LIMIT_BYTES = 56 * 1024 * 1024

D_QK = KV_RANK + QK_ROPE
D_IN_PAD = 2048
COL_QA = 3 * D_CONV
COL_KV = COL_QA + Q_RANK
FF_SPLITS = ((0, 1280), (1280, 1536))
FF_CHUNK = max(w for _, w in FF_SPLITS)
assert sum(w for _, w in FF_SPLITS) == D_FF and all(o % MXU_TILE == 0 for o, _ in FF_SPLITS)
ATTN_UNROLL = 2
ATTN_COL_GROUPS = 1
VT_ROWS = KV_RANK + BF16_SUBLANES
HALO = SUBLANES
SAMPLE_SEQS_PER_STEP = 16

BF16 = jnp.bfloat16
F32 = jnp.float32


def _rms(x, g):
    return x * lax.rsqrt(jnp.mean(x * x, axis=-1, keepdims=True) + EPS) * g


def _dot(a, b):
    return jnp.dot(a, b, preferred_element_type=F32)


def _dot_nt(a, b):
    return lax.dot_general(a, b, (((1,), (1,)), ((), ())), preferred_element_type=F32)


def _chunk_of(pos):
    return lax.shift_right_logical(pos, CHUNK.bit_length() - 1)


def _rope(x, rc, rs1, rs2):
    return x * rc + pltpu.roll(x, LANES - QK_ROPE // 2, 1) * rs1 + pltpu.roll(x, QK_ROPE // 2, 1) * rs2


def _stage_halo(buf_ref, cols, prev):
    n_prev = prev.shape[1]
    buf_ref[:, 0:HALO - n_prev, cols] = jnp.zeros((prev.shape[0], HALO - n_prev, prev.shape[2]), prev.dtype)
    buf_ref[:, HALO - n_prev:HALO, cols] = prev


def _causal_conv(buf_ref, cur, prev, w):
    tt = cur.shape[1]
    _stage_halo(buf_ref, slice(None), prev)
    buf_ref[:, HALO:HALO + tt, :] = cur
    ext = buf_ref[...]
    s1 = pltpu.roll(ext, 1, 1)[:, HALO:HALO + tt, :]
    s2 = pltpu.roll(ext, 2, 1)[:, HALO:HALO + tt, :]
    return s2 * w[0] + s1 * w[1] + cur * w[2]


def _mix_in_kernel(x_ref, cs_ref, w1_ref, wconv_ref, gpre_ref, gqa_ref, wuq_ref, gkva_ref, wukp_ref,
                   rc_ref, rs1_ref, rs2_ref,
                   yconv_ref, q_ref, kcat_ref, ckv_ref, kpe_ref, cso_ref, *rest, nseq, tt, vt_tile, qt_tile):
    cbuf_ref = rest[-1]
    rows = nseq * tt

    @pl.when(pl.program_id(1) == 0)
    def _():
        cso_ref[...] = cs_ref[...]

    x = x_ref[...].reshape(rows, D_MODEL)
    h = _rms(x, gpre_ref[...]).astype(BF16)

    qa = _dot(h, w1_ref[:, COL_QA:COL_KV])
    kv = _dot(h, w1_ref[:, COL_KV:D_IN_PAD])
    xv = _dot(h, w1_ref[:, 0:D_CONV])
    gc = _dot(h, w1_ref[:, 2 * D_CONV:3 * D_CONV])
    gb = _dot(h, w1_ref[:, D_CONV:2 * D_CONV])
    qn = _rms(qa, gqa_ref[...]).astype(BF16)
    q = _dot(qn, wuq_ref[...])
    pair = 2 * QK_NOPE
    qls = [_dot(q[:, pair * p:pair * (p + 1)].astype(BF16), wukp_ref[p]) * Q_SCALE for p in range(N_HEADS // 2)]

    c = (gc * xv).reshape(nseq, tt, D_CONV)
    conv = _causal_conv(cbuf_ref, c, cso_ref[...], wconv_ref[...])
    yconv_ref[...] = (gb.reshape(nseq, tt, D_CONV) * conv).astype(BF16)
    cso_ref[...] = c[:, tt - 2:tt, :]

    rc = rc_ref[...]
    rs1 = rs1_ref[...]
    rs2 = rs2_ref[...]

    n_nope = N_HEADS * QK_NOPE
    heads_per_tile = LANES // QK_ROPE
    for p in range(N_HEADS // 2):
        ql = qls[p]
        for j in range(2):
            hd = 2 * p + j
            qh = ql[:, KV_RANK * j:KV_RANK * (j + 1)]
            if qt_tile:
                qh_t = qh.T.astype(BF16)
                for i in range(tt // qt_tile):
                    q_ref[0, i, 0:KV_RANK, qt_tile * hd:qt_tile * (hd + 1)] = (
                        qh_t[:, qt_tile * i:qt_tile * (i + 1)])
            else:
                q_ref[:, hd, :, 0:KV_RANK] = qh.reshape(nseq, tt, KV_RANK).astype(BF16)
    for half in range(N_HEADS // heads_per_tile):
        qp = _rope(q[:, n_nope + LANES * half:n_nope + LANES * (half + 1)], rc, rs1, rs2) * Q_SCALE
        qp_t = qp.T.astype(BF16) if qt_tile else None
        for j in range(heads_per_tile):
            hd = heads_per_tile * half + j
            if qt_tile:
                for i in range(tt // qt_tile):
                    q_ref[0, i, KV_RANK:D_QK, qt_tile * hd:qt_tile * (hd + 1)] = (
                        qp_t[QK_ROPE * j:QK_ROPE * (j + 1), qt_tile * i:qt_tile * (i + 1)])
            else:
                q_ref[:, hd, :, KV_RANK:D_QK] = (
                    qp[:, QK_ROPE * j:QK_ROPE * (j + 1)].reshape(nseq, tt, QK_ROPE).astype(BF16))

    ckv = _rms(kv[:, 0:KV_RANK], gkva_ref[...])
    kpe = _rope(kv[:, KV_RANK:2 * KV_RANK], rc, rs1, rs2)[:, 0:QK_ROPE]
    ckv_ref[...] = ckv.reshape(nseq, tt, KV_RANK)
    kpe_ref[...] = kpe.reshape(nseq, tt, QK_ROPE)
    kcat_ref[:, :, 0:KV_RANK] = ckv.reshape(nseq, tt, KV_RANK).astype(BF16)
    kcat_ref[:, :, KV_RANK:D_QK] = kpe.reshape(nseq, tt, QK_ROPE).astype(BF16)
    if vt_tile:
        vt_ref = rest[0]
        for i in range(tt // vt_tile):
            vt_ref[0, i, 0:KV_RANK, :] = ckv[vt_tile * i:vt_tile * (i + 1), :].T.astype(BF16)
            vt_ref[0, i, KV_RANK:VT_ROWS, :] = jnp.ones((VT_ROWS - KV_RANK, vt_tile), BF16)


def _const_spec(shape):
    nd = len(shape)
    return pl.BlockSpec(shape, lambda *_: (0,) * nd, pipeline_mode=pl.Buffered(1))


def _mix_in(x, conv_state, lw, ropes, *, nseq, tt, vt_tile=0, qt_tile=0):
    B, T, _ = x.shape
    assert not vt_tile or (nseq == 1 and tt % vt_tile == 0)
    assert not qt_tile or (nseq == 1 and tt % qt_tile == 0)
    if qt_tile:
        q_shape = jax.ShapeDtypeStruct((B, T // qt_tile, D_QK, N_HEADS * qt_tile), BF16)
        q_spec = pl.BlockSpec((1, tt // qt_tile, D_QK, N_HEADS * qt_tile), lambda b, t: (b, t, 0, 0))
    else:
        q_shape = jax.ShapeDtypeStruct((B, N_HEADS, T, D_QK), BF16)
        q_spec = pl.BlockSpec((nseq, N_HEADS, tt, D_QK), lambda b, t: (b, 0, t, 0))
    nb, nt = B // nseq, T // tt
    rows = nseq * tt
    rope_spec = pl.BlockSpec((rows, LANES), lambda b, t: (t, 0))
    in_specs = [
        pl.BlockSpec((nseq, tt, D_MODEL), lambda b, t: (b, t, 0)),
        pl.BlockSpec((nseq, CONV_W - 1, D_CONV), lambda b, t: (b, 0, 0)),
        _const_spec((D_MODEL, D_IN_PAD)),
        _const_spec((CONV_W, D_CONV)),
        _const_spec((1, D_MODEL)),
        _const_spec((1, Q_RANK)),
        _const_spec((Q_RANK, N_HEADS * (QK_NOPE + QK_ROPE))),
        _const_spec((1, KV_RANK)),
        _const_spec((N_HEADS // 2, 2 * QK_NOPE, 2 * KV_RANK)),
        rope_spec, rope_spec, rope_spec,
    ]
    out_shape = [
        jax.ShapeDtypeStruct((B, T, D_CONV), BF16),
        q_shape,
        jax.ShapeDtypeStruct((B, T, D_QK), BF16),
        jax.ShapeDtypeStruct((B, T, KV_RANK), F32),
        jax.ShapeDtypeStruct((B, T, QK_ROPE), F32),
        jax.ShapeDtypeStruct((B, CONV_W - 1, D_CONV), F32),
    ]
    out_specs = [
        pl.BlockSpec((nseq, tt, D_CONV), lambda b, t: (b, t, 0)),
        q_spec,
        pl.BlockSpec((nseq, tt, D_QK), lambda b, t: (b, t, 0)),
        pl.BlockSpec((nseq, tt, KV_RANK), lambda b, t: (b, t, 0)),
        pl.BlockSpec((nseq, tt, QK_ROPE), lambda b, t: (b, t, 0)),
        pl.BlockSpec((nseq, CONV_W - 1, D_CONV), lambda b, t: (b, 0, 0)),
    ]
    if vt_tile:
        out_shape.append(jax.ShapeDtypeStruct((B, T // vt_tile, VT_ROWS, vt_tile), BF16))
        out_specs.append(pl.BlockSpec((1, tt // vt_tile, VT_ROWS, vt_tile), lambda b, t: (b, t, 0, 0)))
    return pl.pallas_call(
        functools.partial(_mix_in_kernel, nseq=nseq, tt=tt, vt_tile=vt_tile, qt_tile=qt_tile),
        grid=(nb, nt),
        in_specs=in_specs,
        out_specs=out_specs,
        out_shape=out_shape,
        scratch_shapes=[pltpu.VMEM((nseq, tt + HALO, D_CONV), F32)],
        compiler_params=pltpu.CompilerParams(
            dimension_semantics=("arbitrary", "arbitrary"), vmem_limit_bytes=VMEM_LIMIT_BYTES),
        name="mix_in",
    )(x, conv_state, lw["w1"], lw["w_conv"], lw["g_mix_pre"], lw["g_qa"], lw["w_uq"], lw["g_kva"],
      lw["w_ukp"], *ropes)


def _attend_prompt_kernel(q_ref, qn_ref, k_ref, vt_ref, o_ref, m_ref, acc_ref,
                          s0_ref, sa_ref, sb_ref, s0_max_ref, sa_max_ref, sb_max_ref, *, tq, tk):
    cols = N_HEADS * tq
    gw = cols // ATTN_COL_GROUPS
    qi = pl.program_id(1)
    q0 = qi * tq
    n_last = q0 // tk

    m_ref[...] = jnp.full(m_ref.shape, -jnp.inf, F32)
    acc_ref[...] = jnp.zeros(acc_ref.shape, F32)

    def scores(j, buf, query_ref=q_ref):
        s_ref, smax_ref = buf
        k0 = pl.multiple_of(j * tk, tk)
        s = _dot(k_ref[0, pl.ds(k0, tk), :], query_ref[0, 0])
        s_ref[...] = s
        smax_ref[...] = jnp.max(s, axis=0, keepdims=True)

    def consume(j, buf, masked):
        s_ref, smax_ref = buf
        vt = vt_ref[0, j]
        for g in range(ATTN_COL_GROUPS):
            sl = slice(gw * g, gw * (g + 1))
            s = s_ref[:, sl]
            if masked:
                k_pos = j * tk + lax.broadcasted_iota(jnp.int32, (tk, gw), 0)
                q_pos = q0 + (lax.broadcasted_iota(jnp.int32, (tk, gw), 1) & (tq - 1))
                s = jnp.where(_chunk_of(k_pos) <= _chunk_of(q_pos), s, -jnp.inf)
                s_max = jnp.max(s, axis=0, keepdims=True)
            else:
                s_max = smax_ref[:, sl]
            m_prev = m_ref[:, sl]
            m_new = jnp.maximum(m_prev, s_max)
            p = jnp.exp2(s - m_new).astype(BF16)
            acc_ref[:, sl] = jnp.exp2(m_prev - m_new) * acc_ref[:, sl] + _dot(vt, p)
            m_ref[:, sl] = m_new

    first = (s0_ref, s0_max_ref)
    bufs = ((sa_ref, sa_max_ref), (sb_ref, sb_max_ref))

    def score_next_query_tile():
        scores(0, first, qn_ref)

    def run(start, count):
        for t in range(count):
            scores(start + t + 1, bufs[(t + 1) % 2])
            consume(start + t, bufs[t % 2], False)

    @pl.when(qi == 0)
    def _():
        scores(0, first)

    @pl.when(n_last == 0)
    def _():
        consume(0, first, True)
        score_next_query_tile()

    @pl.when(n_last > 0)
    def _():
        scores(1, bufs[0])
        consume(0, first, False)

    later = jnp.maximum(n_last - 1, 0)

    def unrolled_body(i, carry):
        run(1 + ATTN_UNROLL * i, ATTN_UNROLL)
        return carry

    lax.fori_loop(0, later // ATTN_UNROLL, unrolled_body, 0)
    done = (later // ATTN_UNROLL) * ATTN_UNROLL
    for rest in range(ATTN_UNROLL):
        @pl.when((n_last > 0) & (later - done == rest))
        def _():
            run(1 + done, rest)
            score_next_query_tile()
            consume(n_last, bufs[rest % 2], True)

    o = acc_ref[0:KV_RANK, :] / acc_ref[KV_RANK:KV_RANK + 1, :]
    for h in range(N_HEADS):
        o_ref[0, :, KV_RANK * h:KV_RANK * (h + 1)] = o[:, tq * h:tq * (h + 1)].T.astype(BF16)


def _attend_prompt(q_t, kcat, vt):
    B, T, _ = kcat.shape
    tk = vt.shape[3]
    cols = q_t.shape[3]
    tq = cols // N_HEADS
    nq = T // tq
    assert tq % CHUNK == 0 and tq & (tq - 1) == 0 and T % tk == 0 and tk % tq == 0
    score_buf = pltpu.VMEM((tk, cols), F32)
    max_buf = pltpu.VMEM((1, cols), F32)
    return pl.pallas_call(
        functools.partial(_attend_prompt_kernel, tq=tq, tk=tk),
        grid=(B, nq),
        in_specs=[
            pl.BlockSpec((1, 1, D_QK, cols), lambda b, i: (b, i, 0, 0)),
            pl.BlockSpec((1, 1, D_QK, cols), lambda b, i: (b, jnp.minimum(i + 1, nq - 1), 0, 0)),
            pl.BlockSpec((1, T, D_QK), lambda b, i: (b, 0, 0)),
            pl.BlockSpec((1, T // tk, VT_ROWS, tk), lambda b, i: (b, 0, 0, 0)),
        ],
        out_specs=pl.BlockSpec((1, tq, N_HEADS * KV_RANK), lambda b, i: (b, i, 0)),
        out_shape=jax.ShapeDtypeStruct((B, T, N_HEADS * KV_RANK), BF16),
        scratch_shapes=[
            pltpu.VMEM((1, cols), F32),
            pltpu.VMEM((VT_ROWS, cols), F32),
            score_buf, score_buf, score_buf,
            max_buf, max_buf, max_buf,
        ],
        compiler_params=pltpu.CompilerParams(
            dimension_semantics=("arbitrary", "arbitrary"), vmem_limit_bytes=VMEM_LIMIT_BYTES),
        name="attend_prompt",
    )(q_t, q_t, kcat, vt)


def _attend_sample_kernel(q_ref, ckv_ref, kpe_ref, kn_ref, o_ref, *, ts, past):
    rows = N_HEADS * ts
    q = q_ref[0].reshape(rows, D_QK)
    ck = ckv_ref[0, 0].astype(BF16)
    kp_t = kpe_ref[0, 0].astype(BF16)
    kn = kn_ref[0]
    s_old = _dot_nt(q[:, 0:KV_RANK], ck) + _dot(q[:, KV_RANK:D_QK], kp_t)
    s_new = _dot_nt(q, kn)
    q_pos = past + (lax.broadcasted_iota(jnp.int32, (rows, ts), 0) & (ts - 1))
    k_pos = past + lax.broadcasted_iota(jnp.int32, (rows, ts), 1)
    s_new = jnp.where(_chunk_of(k_pos) <= _chunk_of(q_pos), s_new, -jnp.inf)
    m = jnp.maximum(jnp.max(s_old, axis=1, keepdims=True), jnp.max(s_new, axis=1, keepdims=True))
    p_old = jnp.exp2(s_old - m)
    p_new = jnp.exp2(s_new - m)
    l = jnp.sum(p_old, axis=1, keepdims=True) + jnp.sum(p_new, axis=1, keepdims=True)
    o = (_dot(p_old.astype(BF16), ck) + _dot(p_new.astype(BF16), kn[:, 0:KV_RANK])) / l
    for h in range(N_HEADS):
        o_ref[0, :, KV_RANK * h:KV_RANK * (h + 1)] = o[ts * h:ts * (h + 1), :].astype(BF16)


def _attend_sample(q, cache_ckv, cache_kpe_t, kcat, layer):
    B, _, ts, _ = q.shape
    past = cache_ckv.shape[2]
    return pl.pallas_call(
        functools.partial(_attend_sample_kernel, ts=ts, past=past),
        grid=(B,),
        in_specs=[
            pl.BlockSpec((1, N_HEADS, ts, D_QK), lambda b: (b, 0, 0, 0)),
            pl.BlockSpec((1, 1, past, KV_RANK), lambda b: (layer, b, 0, 0)),
            pl.BlockSpec((1, 1, QK_ROPE, past), lambda b: (layer, b, 0, 0)),
            pl.BlockSpec((1, ts, D_QK), lambda b: (b, 0, 0)),
        ],
        out_specs=pl.BlockSpec((1, ts, N_HEADS * KV_RANK), lambda b: (b, 0, 0)),
        out_shape=jax.ShapeDtypeStruct((B, ts, N_HEADS * KV_RANK), BF16),
        compiler_params=pltpu.CompilerParams(
            dimension_semantics=("arbitrary",), vmem_limit_bytes=VMEM_LIMIT_BYTES),
        name="attend_sample",
    )(q, cache_ckv, cache_kpe_t, kcat)


def _post_kernel(x_ref, yconv_ref, olat_ref, fs_ref, wuvg_ref, wo_ref, gpost_ref, gfpre_ref, wup_ref,
                 wfc_ref, bfc_ref, wdown_ref, gfpost_ref,
                 xo_ref, fso_ref,
                 ubuf_ref, vbuf_ref, acc_ref, hf_ref, *, nseq, tt):
    rows = nseq * tt

    @pl.when(pl.program_id(1) == 0)
    def _():
        fso_ref[...] = fs_ref[...]

    half = D_MODEL // 2
    ol = olat_ref[...].reshape(rows, N_HEADS * KV_RANK)
    for g in range(2):
        hf_ref[:, D_CONV + 4 * V_HEAD * g:D_CONV + 4 * V_HEAD * (g + 1)] = (
            _dot(ol[:, half * g:half * (g + 1)], wuvg_ref[g]).astype(BF16))
    hf_ref[:, 0:D_CONV] = yconv_ref[...].reshape(rows, D_CONV)
    mix = _dot(hf_ref[...], wo_ref[...])
    x1 = x_ref[...].reshape(rows, D_MODEL) + _rms(mix, gpost_ref[...])
    xo_ref[...] = x1.reshape(nseq, tt, D_MODEL)
    hf_ref[...] = _rms(x1, gfpre_ref[...]).astype(BF16)
    acc_ref[...] = jnp.zeros(acc_ref.shape, F32)

    def up_proj(o, w, buf_ref):
        for part in range(2):
            src = slice(D_FF * part + o, D_FF * part + o + w)
            dst = slice(FF_CHUNK * part, FF_CHUNK * part + w)
            _stage_halo(buf_ref, dst, fso_ref[:, :, src])
            up = _dot(hf_ref[...], wup_ref[:, src]).reshape(nseq, tt, w)
            buf_ref[:, HALO:HALO + tt, dst] = up
            fso_ref[:, :, src] = up[:, tt - 2:tt, :]

    def gate_down(o, w, buf_ref):
        u = []
        for part in range(2):
            src = slice(D_FF * part + o, D_FF * part + o + w)
            dst = slice(FF_CHUNK * part, FF_CHUNK * part + w)
            wc = wfc_ref[:, src]
            ext = buf_ref[:, :, dst]
            s1 = pltpu.roll(ext, 1, 1)[:, HALO:HALO + tt, :]
            s2 = pltpu.roll(ext, 2, 1)[:, HALO:HALO + tt, :]
            u.append((s2 * wc[0] + s1 * wc[1] + ext[:, HALO:HALO + tt, :] * wc[2] + bfc_ref[:, src]).reshape(rows, w))
        a, b = u
        gated = (a * (1.0 / (1.0 + jnp.exp(-a))) * b).astype(BF16)
        acc_ref[...] += _dot(gated, wdown_ref[o:o + w, :])

    bufs = (ubuf_ref, vbuf_ref)
    up_proj(*FF_SPLITS[0], bufs[0])
    for c, (o, w) in enumerate(FF_SPLITS):
        if c + 1 < len(FF_SPLITS):
            up_proj(*FF_SPLITS[c + 1], bufs[(c + 1) % 2])
        gate_down(o, w, bufs[c % 2])
    xo_ref[...] = xo_ref[...] + _rms(acc_ref[...], gfpost_ref[...]).reshape(nseq, tt, D_MODEL)


def _post(x, yconv, olat, ffn_state, lw, *, nseq, tt):
    B, T, _ = x.shape
    nb, nt = B // nseq, T // tt
    rows = nseq * tt
    state_block = (nseq, CONV_W - 1, 2 * D_FF)
    in_specs = [
        pl.BlockSpec((nseq, tt, D_MODEL), lambda b, t: (b, t, 0)),
        pl.BlockSpec((nseq, tt, D_CONV), lambda b, t: (b, t, 0)),
        pl.BlockSpec((nseq, tt, N_HEADS * KV_RANK), lambda b, t: (b, t, 0)),
        pl.BlockSpec(state_block, lambda b, t: (b, 0, 0), pipeline_mode=pl.Buffered(1)),
        _const_spec((2, 4 * KV_RANK, 4 * V_HEAD)),
        _const_spec((D_MODEL, D_MODEL)),
        _const_spec((1, D_MODEL)),
        _const_spec((1, D_MODEL)),
        _const_spec((D_MODEL, 2 * D_FF)),
        _const_spec((CONV_W, 2 * D_FF)),
        _const_spec((1, 2 * D_FF)),
        _const_spec((D_FF, D_MODEL)),
        _const_spec((1, D_MODEL)),
    ]
    return pl.pallas_call(
        functools.partial(_post_kernel, nseq=nseq, tt=tt),
        grid=(nb, nt),
        in_specs=in_specs,
        out_specs=(
            pl.BlockSpec((nseq, tt, D_MODEL), lambda b, t: (b, t, 0)),
            pl.BlockSpec(state_block, lambda b, t: (b, 0, 0)),
        ),
        out_shape=(
            jax.ShapeDtypeStruct((B, T, D_MODEL), F32),
            jax.ShapeDtypeStruct((B, CONV_W - 1, 2 * D_FF), F32),
        ),
        scratch_shapes=[
            pltpu.VMEM((nseq, tt + HALO, 2 * FF_CHUNK), F32),
            pltpu.VMEM((nseq, tt + HALO, 2 * FF_CHUNK), F32),
            pltpu.VMEM((rows, D_MODEL), F32),
            pltpu.VMEM((rows, D_MODEL), BF16),
        ],
        compiler_params=pltpu.CompilerParams(
            dimension_semantics=("arbitrary", "arbitrary"), vmem_limit_bytes=VMEM_LIMIT_BYTES),
        name="post_ffn",
    )(x, yconv, olat, ffn_state, lw["w_uvg"], lw["w_o"], lw["g_mix_post"], lw["g_ffn_pre"], lw["w_up"],
      lw["w_ffn_conv"], lw["b_ffn_conv"], lw["w_down"], lw["g_ffn_post"])


def _block_diag(blocks):
    n = len(blocks)
    rows = []
    for i, blk in enumerate(blocks):
        rows.append(jnp.concatenate(
            [blk if j == i else jnp.zeros((blk.shape[0], blocks[j].shape[1]), blk.dtype) for j in range(n)],
            axis=1))
    return jnp.concatenate(rows, axis=0)


def _layer_weights(l, w_in, w_conv, g_qa, w_uq, g_kva, w_uk, w_uv, w_o, g_mix_pre, g_mix_post,
                   w_up, w_ffn_conv, b_ffn_conv, w_down, g_ffn_pre, g_ffn_post):
    d_in = w_in.shape[2]
    uq = w_uq[l].reshape(Q_RANK, N_HEADS, QK_NOPE + QK_ROPE)
    uq = jnp.concatenate([uq[:, :, :QK_NOPE].reshape(Q_RANK, N_HEADS * QK_NOPE),
                          uq[:, :, QK_NOPE:].reshape(Q_RANK, N_HEADS * QK_ROPE)], axis=1)
    uk_t = jnp.transpose(w_uk[l], (1, 2, 0))
    uv = jnp.transpose(w_uv[l], (1, 0, 2))
    return {
        "w1": jnp.pad(w_in[l], ((0, 0), (0, D_IN_PAD - d_in))).astype(BF16),
        "w_conv": w_conv[l],
        "g_mix_pre": g_mix_pre[l][None],
        "g_qa": g_qa[l][None],
        "w_uq": uq.astype(BF16),
        "g_kva": g_kva[l][None],
        "w_ukp": jnp.stack([_block_diag([uk_t[2 * p], uk_t[2 * p + 1]])
                            for p in range(N_HEADS // 2)]).astype(BF16),
        "w_uvg": jnp.stack([_block_diag([uv[4 * g + j] for j in range(4)]) for g in range(2)]).astype(BF16),
        "w_o": w_o[l].astype(BF16),
        "g_mix_post": g_mix_post[l][None],
        "g_ffn_pre": g_ffn_pre[l][None],
        "w_up": w_up[l].astype(BF16),
        "w_ffn_conv": w_ffn_conv[l],
        "b_ffn_conv": b_ffn_conv[l][None],
        "w_down": w_down[l].astype(BF16),
        "g_ffn_post": g_ffn_post[l][None],
    }


def _rope_tables(pos, reps):
    inv = 1.0 / (ROPE_BASE ** (jnp.arange(0, QK_ROPE, 2, dtype=F32) / QK_ROPE))
    ang = pos.astype(F32)[:, None] * inv[None, :]
    cos, sin = jnp.cos(ang), jnp.sin(ang)
    zero = jnp.zeros_like(sin)
    group = LANES // QK_ROPE
    rc = jnp.tile(jnp.concatenate([cos, cos], axis=1), (reps, group))
    rs1 = jnp.tile(jnp.concatenate([-sin, zero], axis=1), (reps, group))
    rs2 = jnp.tile(jnp.concatenate([zero, sin], axis=1), (reps, group))
    return rc, rs1, rs2


def _pick_tile(T, target):
    tt = min(T, target)
    assert T % tt == 0 and tt % BF16_SUBLANES == 0
    return tt


def _trunk(x_prompt, x_sample, cache_ckv, cache_kpe, state_conv, state_ffn, weights):
    depth = cache_ckv.shape[0]
    Bp, Tp, _ = x_prompt.shape
    Bs, Ts, _ = x_sample.shape
    past = cache_ckv.shape[2]
    assert Ts >= CONV_W - 1 and Ts % BF16_SUBLANES == 0

    tt = _pick_tile(Tp, 512)
    tt_in = _pick_tile(Tp, 1024)
    tq = _pick_tile(Tp, 256)
    tk = _pick_tile(Tp, 512)
    ns = math.gcd(Bs, SAMPLE_SEQS_PER_STEP)
    p_ropes = _rope_tables(jnp.arange(Tp), 1)
    s_ropes = _rope_tables(past + jnp.arange(Ts), ns)
    p_conv0 = jnp.zeros((Bp, CONV_W - 1, D_CONV), F32)
    p_ffn0 = jnp.zeros((Bp, CONV_W - 1, 2 * D_FF), F32)

    cache_kpe_t = jnp.swapaxes(cache_kpe, 2, 3)
    xp, xs = x_prompt, x_sample
    outs = [[] for _ in range(8)]
    for l in range(depth):
        lw = _layer_weights(l, *weights)
        yc_p, q_p, kcat_p, ckv_p, kpe_p, conv_p, vt_p = _mix_in(
            xp, p_conv0, lw, p_ropes, nseq=1, tt=tt_in, vt_tile=tk, qt_tile=tq)
        yc_s, q_s, kcat_s, ckv_s, kpe_s, conv_s = _mix_in(xs, state_conv[l], lw, s_ropes, nseq=ns, tt=Ts)
        ol_p = _attend_prompt(q_p, kcat_p, vt_p)
        ol_s = _attend_sample(q_s, cache_ckv, cache_kpe_t, kcat_s, l)
        xp, ffn_p = _post(xp, yc_p, ol_p, p_ffn0, lw, nseq=1, tt=tt)
        xs, ffn_s = _post(xs, yc_s, ol_s, state_ffn[l], lw, nseq=ns, tt=Ts)
        for acc, v in zip(outs, (ckv_p, kpe_p, conv_p, ffn_p, ckv_s, kpe_s, conv_s, ffn_s)):
            acc.append(v)
    return (xp, xs) + tuple(jnp.stack(v) for v in outs)


def kernel(x_prompt, x_sample, cache_ckv, cache_kpe, state_conv, state_ffn, w_in, w_conv, g_qa, w_uq, g_kva, w_uk, w_uv, w_o, g_mix_pre, g_mix_post, w_up, w_ffn_conv, b_ffn_conv, w_down, g_ffn_pre, g_ffn_post):
    weights = (w_in, w_conv, g_qa, w_uq, g_kva, w_uk, w_uv, w_o, g_mix_pre, g_mix_post,
               w_up, w_ffn_conv, b_ffn_conv, w_down, g_ffn_pre, g_ffn_post)
    return _trunk(x_prompt, x_sample, cache_ckv, cache_kpe, state_conv, state_ffn, weights)
```
